```python
import jax, jax.numpy as jnp
from jax import lax
import numpy as np

D_MODEL = 1024
BATCH = 16
SEQ = 2048
DEPTH = 2
DEC_BATCH = 8
DEC_SEQ = 4096
PAST_LEN = 128

N_HEADS = 8
N_KV_HEADS = 2
HEAD_DIM = 128
GROUP = N_HEADS // N_KV_HEADS
ATTN_W = N_HEADS * HEAD_DIM
KV_W = N_KV_HEADS * HEAD_DIM
Q_BLOCK = 128
ROPE_THETA = 10000.0
ROPE_AXIS_DIM = HEAD_DIM // 2
GRID_W = 64
D_CONV = 1024
CONV_K = 31
N_BRANCH = 2
D_FF = -(-8 * D_MODEL // (3 * 256)) * 256
EPS = 1e-6
D_IN = ATTN_W + 2 * KV_W + 2 * D_CONV + N_BRANCH * D_MODEL

kernel_name = "hybrid_conv_axial_gqa_encoder"


def rms_norm(x, g):
    xf = x.astype(jnp.float32)
    y = xf * lax.rsqrt(jnp.mean(xf * xf, axis=-1, keepdims=True) + EPS)
    return (y * g.astype(jnp.float32)).astype(x.dtype)


def layer_norm(x, g, b):
    xf = x.astype(jnp.float32)
    mu = jnp.mean(xf, axis=-1, keepdims=True)
    xc = xf - mu
    var = jnp.mean(xc * xc, axis=-1, keepdims=True)
    y = xc * lax.rsqrt(var + EPS)
    return (y * g.astype(jnp.float32) + b.astype(jnp.float32)).astype(x.dtype)


def axial_rope_tables(seq_len, dtype):
    rows = seq_len // GRID_W
    row = jnp.repeat(jnp.arange(rows), GRID_W).astype(jnp.float32)
    col = jnp.tile(jnp.arange(GRID_W), rows).astype(jnp.float32)
    half = ROPE_AXIS_DIM // 2
    inv = ROPE_THETA ** (-jnp.arange(half, dtype=jnp.float32) / half)
    ang = jnp.stack([row[:, None] * inv, col[:, None] * inv], axis=1)
    ang = ang[:, None]
    return jnp.cos(ang).astype(dtype), jnp.sin(ang).astype(dtype)


def apply_axial_rope(x, cos, sin):
    B, S, H, D = x.shape
    xr = x.reshape(B, S, H, 2, 2, ROPE_AXIS_DIM // 2)
    x1, x2 = xr[..., 0, :], xr[..., 1, :]
    o1 = x1 * cos - x2 * sin
    o2 = x2 * cos + x1 * sin
    return jnp.stack([o1, o2], axis=-2).reshape(B, S, H, D)


def blocked_attention(q, k, v):
    B, S = q.shape[0], q.shape[1]
    nblk = S // Q_BLOCK
    qb = q.reshape(B, nblk, Q_BLOCK, N_KV_HEADS, GROUP, HEAD_DIM).transpose(1, 0, 2, 3, 4, 5)
    scale = HEAD_DIM ** -0.5

    def one_block(qblk):
        s = jnp.einsum('bqkgd,bskd->bkgqs', qblk, k).astype(jnp.float32) * scale
        p = jax.nn.softmax(s, axis=-1).astype(v.dtype)
        return jnp.einsum('bkgqs,bskd->bqkgd', p, v)

    o = lax.map(one_block, qb)
    return o.transpose(1, 0, 2, 3, 4, 5).reshape(B, S, ATTN_W)


def conv_module(u, w_dw, b_dw, ln_g, ln_b, w_pw):
    a, g = jnp.split(u, 2, axis=-1)
    z = a * jax.nn.sigmoid(g)
    z = lax.conv_general_dilated(
        z, w_dw[:, None, :].astype(z.dtype), window_strides=(1,),
        padding=[(CONV_K // 2, CONV_K // 2)],
        dimension_numbers=('NWC', 'WIO', 'NWC'),
        feature_group_count=D_CONV) + b_dw
    z = jax.nn.silu(layer_norm(z, ln_g, ln_b))
    return z @ w_pw


def trunk(x, g_mix, w_in, g_q, g_k, w_dw, b_dw, ln_g, ln_b, w_pw, b_gate,
          w_out, g_ffn, w_gu, w_down, g_final):
    B, S, _ = x.shape
    cos, sin = axial_rope_tables(S, x.dtype)
    splits = [ATTN_W, ATTN_W + KV_W, ATTN_W + 2 * KV_W, ATTN_W + 2 * KV_W + 2 * D_CONV]
    for l in range(DEPTH):
        h = rms_norm(x, g_mix[l])
        proj = h @ w_in[l]
        q, k, v, u, gl = jnp.split(proj, splits, axis=-1)
        q = rms_norm(q.reshape(B, S, N_HEADS, HEAD_DIM), g_q[l])
        k = rms_norm(k.reshape(B, S, N_KV_HEADS, HEAD_DIM), g_k[l])
        q = apply_axial_rope(q, cos, sin)
        k = apply_axial_rope(k, cos, sin)
        v = v.reshape(B, S, N_KV_HEADS, HEAD_DIM)
        a = blocked_attention(q, k, v)
        c = conv_module(u, w_dw[l], b_dw[l], ln_g[l], ln_b[l], w_pw[l])
        gates = jax.nn.sigmoid(gl + b_gate[l])
        ga, gc = jnp.split(gates, 2, axis=-1)
        x = x + (ga * a + gc * c) @ w_out[l]
        h2 = rms_norm(x, g_ffn[l])
        gate, up = jnp.split(h2 @ w_gu[l], 2, axis=-1)
        x = x + (jax.nn.silu(gate) * up) @ w_down[l]
    return rms_norm(x, g_final)


def setup_inputs(seed: int = 0) -> dict:
    key = jax.random.key(seed)
    ks = jax.random.split(key, 20)
    f32 = jnp.float32

    def nrm(k, shape, scale):
        return jax.random.normal(k, shape, f32) * scale

    def gain(k, shape):
        return 1.0 + 0.02 * jax.random.normal(k, shape, f32)

    return {
        "x_prompt": nrm(ks[0], (BATCH, SEQ, D_MODEL), 1.0),
        "x_sample": nrm(ks[1], (DEC_BATCH, DEC_SEQ, D_MODEL), 1.0),
        "g_mix": gain(ks[2], (DEPTH, D_MODEL)),
        "w_in": nrm(ks[3], (DEPTH, D_MODEL, D_IN), D_MODEL ** -0.5),
        "g_q": gain(ks[4], (DEPTH, HEAD_DIM)),
        "g_k": gain(ks[5], (DEPTH, HEAD_DIM)),
        "w_dw": nrm(ks[6], (DEPTH, CONV_K, D_CONV), CONV_K ** -0.5),
        "b_dw": nrm(ks[7], (DEPTH, D_CONV), 0.02),
        "ln_g": gain(ks[8], (DEPTH, D_CONV)),
        "ln_b": nrm(ks[9], (DEPTH, D_CONV), 0.02),
        "w_pw": nrm(ks[10], (DEPTH, D_CONV, D_MODEL), D_CONV ** -0.5),
        "b_gate": nrm(ks[11], (DEPTH, N_BRANCH * D_MODEL), 0.02),
        "w_out": nrm(ks[12], (DEPTH, D_MODEL, D_MODEL), D_MODEL ** -0.5),
        "g_ffn": gain(ks[13], (DEPTH, D_MODEL)),
        "w_gu": nrm(ks[14], (DEPTH, D_MODEL, 2 * D_FF), D_MODEL ** -0.5),
        "w_down": nrm(ks[15], (DEPTH, D_FF, D_MODEL), D_FF ** -0.5),
        "g_final": gain(ks[16], (D_MODEL,)),
    }


def reference(x_prompt, x_sample, g_mix, w_in, g_q, g_k, w_dw, b_dw, ln_g, ln_b,
              w_pw, b_gate, w_out, g_ffn, w_gu, w_down, g_final):
    y_prompt = trunk(x_prompt, g_mix, w_in, g_q, g_k, w_dw, b_dw, ln_g, ln_b, w_pw,
                     b_gate, w_out, g_ffn, w_gu, w_down, g_final)
    y_sample = trunk(x_sample, g_mix, w_in, g_q, g_k, w_dw, b_dw, ln_g, ln_b, w_pw,
                     b_gate, w_out, g_ffn, w_gu, w_down, g_final)
    return (y_prompt, y_sample)
```

```python
import functools

import numpy as np
import jax
import jax.numpy as jnp
from jax import lax
from jax.experimental import pallas as pl
from jax.experimental.pallas import tpu as pltpu

D_MODEL = 1024
N_HEADS = 8
N_KV_HEADS = 2
HEAD_DIM = 128
GROUP = N_HEADS // N_KV_HEADS
ATTN_W = N_HEADS * HEAD_DIM
KV_W = N_KV_HEADS * HEAD_DIM
GROUP_W = GROUP * HEAD_DIM
ROPE_THETA = 10000.0
GRID_W = 64
D_CONV = 1024
CONV_K = 31
CONV_PAD = CONV_K // 2
D_FF = 2816
EPS = 1e-6

HALO = 16
CONV_ROWS = 32
LANES = 128
VMEM_LIMIT = 56 * 1024 * 1024

F32 = jnp.float32
BF16 = jnp.bfloat16


def _rms(x, g):
    return x * lax.rsqrt(jnp.mean(x * x, axis=-1, keepdims=True) + EPS) * g


def _sigmoid(x):
    return 1.0 / (1.0 + jnp.exp(-x))


def _resident(shape):
    nd = len(shape)
    return pl.BlockSpec(shape, lambda *_: (0,) * nd, pipeline_mode=pl.Buffered(1))


def _inproj_kernel(x_ref, gmix_ref, wq_ref, wk_ref, wv_ref, wa_ref, wg_ref, wga_ref, wgc_ref,
                   gq_ref, gk_ref, bga_ref, bgc_ref, cos_ref, sin_ref,
                   q_ref, k_ref, v_ref, z_ref, ga_ref, gc_ref):
    h = _rms(x_ref[...], gmix_ref[...]).astype(BF16)
    cos = cos_ref[...]
    sin = sin_ref[...]

    def norm_rope(t, g):
        t = _rms(t, g)
        return t * cos + pltpu.roll(t, HEAD_DIM // 2, axis=1) * sin

    q = jnp.dot(h, wq_ref[...], preferred_element_type=F32)
    gq = gq_ref[...]
    for hd in range(N_HEADS):
        sl = slice(hd * HEAD_DIM, (hd + 1) * HEAD_DIM)
        q_ref[:, sl] = norm_rope(q[:, sl], gq).astype(BF16)
    k = jnp.dot(h, wk_ref[...], preferred_element_type=F32)
    gk = gk_ref[...]
    for hd in range(N_KV_HEADS):
        sl = slice(hd * HEAD_DIM, (hd + 1) * HEAD_DIM)
        k_ref[:, sl] = norm_rope(k[:, sl], gk).astype(BF16)
    v_ref[...] = jnp.dot(h, wv_ref[...], preferred_element_type=F32).astype(BF16)
    a = jnp.dot(h, wa_ref[...], preferred_element_type=F32)
    g = jnp.dot(h, wg_ref[...], preferred_element_type=F32)
    z_ref[...] = (a * _sigmoid(g)).astype(BF16)
    ga = jnp.dot(h, wga_ref[...], preferred_element_type=F32)
    ga_ref[...] = _sigmoid(ga + bga_ref[...]).astype(BF16)
    gc = jnp.dot(h, wgc_ref[...], preferred_element_type=F32)
    gc_ref[...] = _sigmoid(gc + bgc_ref[...]).astype(BF16)


def _in_proj(x2, lw, cos, sin, seq, tm):
    m = x2.shape[0]
    nseq = seq // tm
    row = lambda w: pl.BlockSpec((tm, w), lambda i: (i, 0))
    rope = pl.BlockSpec((tm, HEAD_DIM), lambda i: (i % nseq, 0))
    weights = [lw["gmix"], lw["wq"], lw["wk"], lw["wv"], lw["wa"], lw["wg"], lw["wga"], lw["wgc"],
               lw["gq"], lw["gk"], lw["bga"], lw["bgc"]]
    out_w = [ATTN_W, KV_W, KV_W, D_CONV, D_MODEL, D_MODEL]
    return pl.pallas_call(
        _inproj_kernel,
        grid=(m // tm,),
        in_specs=[row(D_MODEL)] + [_resident(w.shape) for w in weights] + [rope, rope],
        out_specs=[row(w) for w in out_w],
        out_shape=[jax.ShapeDtypeStruct((m, w), BF16) for w in out_w],
        compiler_params=pltpu.CompilerParams(
            dimension_semantics=("parallel",), vmem_limit_bytes=VMEM_LIMIT),
        name="in_proj",
    )(x2, *weights, cos, sin)


def _attn_kernel(q_ref, k_ref, v_ref, o_ref):
    k = k_ref[...]
    v = v_ref[...]
    for g in range(GROUP):
        sl = slice(g * HEAD_DIM, (g + 1) * HEAD_DIM)
        s = lax.dot_general(q_ref[:, sl], k, (((1,), (1,)), ((), ())),
                            preferred_element_type=F32)
        p = jnp.exp(s - jnp.max(s, axis=-1, keepdims=True))
        l = jnp.sum(p, axis=-1, keepdims=True)
        o = jnp.dot(p.astype(BF16), v, preferred_element_type=F32)
        o_ref[:, sl] = (o * (1.0 / l)).astype(BF16)


def _attention(q, k, v, batch, seq, tq):
    m = q.shape[0]
    nq = seq // tq
    return pl.pallas_call(
        _attn_kernel,
        grid=(batch, N_KV_HEADS, nq),
        in_specs=[
            pl.BlockSpec((tq, GROUP_W), lambda b, h, i: (b * nq + i, h)),
            pl.BlockSpec((seq, HEAD_DIM), lambda b, h, i: (b, h)),
            pl.BlockSpec((seq, HEAD_DIM), lambda b, h, i: (b, h)),
        ],
        out_specs=pl.BlockSpec((tq, GROUP_W), lambda b, h, i: (b * nq + i, h)),
        out_shape=jax.ShapeDtypeStruct((m, ATTN_W), BF16),
        compiler_params=pltpu.CompilerParams(
            dimension_semantics=("parallel", "parallel", "arbitrary"),
            vmem_limit_bytes=VMEM_LIMIT),
        name="attention",
    )(q, k, v)


def _conv_merge_kernel(zp_ref, zc_ref, zn_ref, a_ref, ga_ref, gc_ref, x_ref,
                       wdw_ref, bdw_ref, lng_ref, lnb_ref, wpw_ref, wout_ref,
                       o_ref, win_ref, conv_ref, *, ts):
    i = pl.program_id(1)
    first = i == 0
    last = i == pl.num_programs(1) - 1
    win_ref[0:HALO, :] = jnp.where(first, 0.0, zp_ref[...].astype(F32))
    win_ref[HALO:HALO + ts, :] = zc_ref[...].astype(F32)
    win_ref[HALO + ts:, :] = jnp.where(last, 0.0, zn_ref[...].astype(F32))

    off0 = HALO - CONV_PAD
    for c in range(D_CONV // LANES):
        ln = slice(c * LANES, (c + 1) * LANES)
        taps = [wdw_ref[kk:kk + 1, ln] for kk in range(CONV_K)]
        bias = bdw_ref[:, ln]

        for r in range(ts // CONV_ROWS):
            base = r * CONV_ROWS + off0
            acc = win_ref[base:base + CONV_ROWS, ln] * taps[0]
            for kk in range(1, CONV_K):
                acc = acc + win_ref[base + kk:base + kk + CONV_ROWS, ln] * taps[kk]
            conv_ref[r * CONV_ROWS:(r + 1) * CONV_ROWS, ln] = acc + bias

    y = conv_ref[...]
    mu = jnp.mean(y, axis=-1, keepdims=True)
    yc = y - mu
    var = jnp.mean(yc * yc, axis=-1, keepdims=True)
    y = yc * lax.rsqrt(var + EPS) * lng_ref[...] + lnb_ref[...]
    y = (y * _sigmoid(y)).astype(BF16)
    cbr = jnp.dot(y, wpw_ref[...], preferred_element_type=F32)
    mix = ga_ref[...].astype(F32) * a_ref[...].astype(F32) + gc_ref[...].astype(F32) * cbr
    o_ref[...] = x_ref[...] + jnp.dot(mix.astype(BF16), wout_ref[...], preferred_element_type=F32)


def _conv_merge(z, a, ga, gc, x2, lw, batch, seq, ts):
    m = x2.shape[0]
    ns = seq // ts
    hb = ts // HALO
    nhalo = m // HALO
    row = lambda w: pl.BlockSpec((ts, w), lambda b, i: (b * ns + i, 0))
    prev = pl.BlockSpec((HALO, D_CONV), lambda b, i: (jnp.maximum((b * ns + i) * hb - 1, 0), 0))
    nxt = pl.BlockSpec((HALO, D_CONV),
                       lambda b, i: (jnp.minimum((b * ns + i + 1) * hb, nhalo - 1), 0))
    weights = [lw["wdw"], lw["bdw"], lw["lng"], lw["lnb"], lw["wpw"], lw["wout"]]
    return pl.pallas_call(
        functools.partial(_conv_merge_kernel, ts=ts),
        grid=(batch, ns),
        in_specs=[prev, row(D_CONV), nxt, row(ATTN_W), row(D_MODEL), row(D_MODEL), row(D_MODEL)]
                 + [_resident(w.shape) for w in weights],
        out_specs=row(D_MODEL),
        out_shape=jax.ShapeDtypeStruct((m, D_MODEL), F32),
        scratch_shapes=[pltpu.VMEM((ts + 2 * HALO, D_CONV), F32), pltpu.VMEM((ts, D_CONV), F32)],
        compiler_params=pltpu.CompilerParams(
            dimension_semantics=("parallel", "arbitrary"), vmem_limit_bytes=VMEM_LIMIT),
        name="conv_merge",
    )(z, z, z, a, ga, gc, x2, *weights)


def _ffn_kernel(x_ref, g_ref, wgate_ref, wup_ref, wdown_ref, gfin_ref, o_ref, *, final):
    x = x_ref[...]
    h = _rms(x, g_ref[...]).astype(BF16)
    gate = jnp.dot(h, wgate_ref[...], preferred_element_type=F32)
    up = jnp.dot(h, wup_ref[...], preferred_element_type=F32)
    act = (gate * _sigmoid(gate) * up).astype(BF16)
    y = x + jnp.dot(act, wdown_ref[...], preferred_element_type=F32)
    if final:
        y = _rms(y, gfin_ref[...])
    o_ref[...] = y


def _ffn(x2, lw, gfin, final, tm):
    m = x2.shape[0]
    row = pl.BlockSpec((tm, D_MODEL), lambda i: (i, 0))
    weights = [lw["gffn"], lw["wgate"], lw["wup"], lw["wdown"], gfin]
    return pl.pallas_call(
        functools.partial(_ffn_kernel, final=final),
        grid=(m // tm,),
        in_specs=[row] + [_resident(w.shape) for w in weights],
        out_specs=row,
        out_shape=jax.ShapeDtypeStruct((m, D_MODEL), F32),
        compiler_params=pltpu.CompilerParams(
            dimension_semantics=("parallel",), vmem_limit_bytes=VMEM_LIMIT),
        name="ffn",
    )(x2, *weights)


def _head_perm():
    q = HEAD_DIM // 4
    return np.concatenate([np.arange(0, q), np.arange(2 * q, 3 * q),
                           np.arange(q, 2 * q), np.arange(3 * q, 4 * q)])


def _rope_tables(seq):
    quarter = HEAD_DIM // 4
    t = jnp.arange(seq)
    row = (t // GRID_W).astype(F32)
    col = (t % GRID_W).astype(F32)
    inv = ROPE_THETA ** (-jnp.arange(quarter, dtype=F32) / quarter)
    ar = row[:, None] * inv
    ac = col[:, None] * inv
    cos = jnp.concatenate([jnp.cos(ar), jnp.cos(ac), jnp.cos(ar), jnp.cos(ac)], axis=1)
    sin = jnp.concatenate([-jnp.sin(ar), -jnp.sin(ac), jnp.sin(ar), jnp.sin(ac)], axis=1)
    return cos, sin


def _layer_weights(l, g_mix, w_in, g_q, g_k, w_dw, b_dw, ln_g, ln_b, w_pw, b_gate,
                   w_out, g_ffn, w_gu, w_down):
    perm = _head_perm()
    w = w_in[l]
    o_k, o_v, o_a = ATTN_W, ATTN_W + KV_W, ATTN_W + 2 * KV_W
    o_g, o_ga, o_gc = o_a + D_CONV, o_a + 2 * D_CONV, o_a + 2 * D_CONV + D_MODEL
    qcols = (np.arange(N_HEADS)[:, None] * HEAD_DIM + perm[None, :]).reshape(-1)
    kcols = o_k + (np.arange(N_KV_HEADS)[:, None] * HEAD_DIM + perm[None, :]).reshape(-1)
    r = lambda v: v.reshape(1, -1)
    return dict(
        gmix=r(g_mix[l]),
        wq=w[:, qcols].astype(BF16), wk=w[:, kcols].astype(BF16),
        wv=w[:, o_v:o_a].astype(BF16), wa=w[:, o_a:o_g].astype(BF16),
        wg=w[:, o_g:o_ga].astype(BF16), wga=w[:, o_ga:o_gc].astype(BF16),
        wgc=w[:, o_gc:].astype(BF16),
        gq=r(g_q[l][perm] * HEAD_DIM ** -0.5), gk=r(g_k[l][perm]),
        bga=r(b_gate[l][:D_MODEL]), bgc=r(b_gate[l][D_MODEL:]),
        wdw=w_dw[l], bdw=r(b_dw[l]), lng=r(ln_g[l]), lnb=r(ln_b[l]),
        wpw=w_pw[l].astype(BF16), wout=w_out[l].astype(BF16),
        gffn=r(g_ffn[l]), wgate=w_gu[l][:, :D_FF].astype(BF16),
        wup=w_gu[l][:, D_FF:].astype(BF16), wdown=w_down[l].astype(BF16),
    )


def _tiles(seq):
    return dict(tm=min(seq, 512), tq=min(seq, 256), ts=min(seq, 256), tf=min(seq, 256))


def _trunk(x, layers, gfin):
    batch, seq, _ = x.shape
    t = _tiles(seq)
    cos, sin = _rope_tables(seq)
    x2 = x.reshape(batch * seq, D_MODEL)
    for l, lw in enumerate(layers):
        q, k, v, z, ga, gc = _in_proj(x2, lw, cos, sin, seq, t["tm"])
        a = _attention(q, k, v, batch, seq, t["tq"])
        x2 = _conv_merge(z, a, ga, gc, x2, lw, batch, seq, t["ts"])
        x2 = _ffn(x2, lw, gfin, l == len(layers) - 1, t["tf"])
    return x2.reshape(batch, seq, D_MODEL)


def kernel(x_prompt, x_sample, g_mix, w_in, g_q, g_k, w_dw, b_dw, ln_g, ln_b, w_pw, b_gate,
           w_out, g_ffn, w_gu, w_down, g_final):
    layers = [_layer_weights(l, g_mix, w_in, g_q, g_k, w_dw, b_dw, ln_g, ln_b, w_pw, b_gate,
                             w_out, g_ffn, w_gu, w_down) for l in range(w_in.shape[0])]
    gfin = g_final.reshape(1, -1)
    return (_trunk(x_prompt, layers, gfin), _trunk(x_sample, layers, gfin))
```

```python
import functools

import numpy as np
import jax
import jax.numpy as jnp
from jax import lax
from jax.experimental import pallas as pl
from jax.experimental.pallas import tpu as pltpu

D_MODEL = 1024
N_HEADS = 8
N_KV_HEADS = 2
HEAD_DIM = 128
GROUP = N_HEADS // N_KV_HEADS
ATTN_W = N_HEADS * HEAD_DIM
KV_W = N_KV_HEADS * HEAD_DIM
GROUP_W = GROUP * HEAD_DIM
ROPE_THETA = 10000.0
GRID_W = 64
D_CONV = 1024
CONV_K = 31
CONV_PAD = CONV_K // 2
D_FF = 2816
EPS = 1e-6
LOG2E = 1.4426950408889634

HALO = 16
CONV_ROWS = 128
LANES = 128
SUBLANES = 8
VMEM_LIMIT = 56 * 1024 * 1024

F32 = jnp.float32
BF16 = jnp.bfloat16


def _rms(x, g):
    return x * lax.rsqrt(jnp.mean(x * x, axis=-1, keepdims=True) + EPS) * g


def _sigmoid(x):
    return 1.0 / (1.0 + jnp.exp(-x))


def _resident(shape):
    nd = len(shape)
    return pl.BlockSpec(shape, lambda *_: (0,) * nd, pipeline_mode=pl.Buffered(1))


def _inproj_kernel(x_ref, gmix_ref, wq_ref, wk_ref, wv_ref, wa_ref, wg_ref, wga_ref, wgc_ref,
                   gq_ref, gk_ref, bga_ref, bgc_ref, cos_ref, sin_ref,
                   q_ref, k_ref, vt_ref, z_ref, ga_ref, gc_ref):
    h = _rms(x_ref[...], gmix_ref[...]).astype(BF16)
    cos = cos_ref[...]
    sin = sin_ref[...]

    def norm_rope(t, g):
        t = _rms(t, g)
        return t * cos + pltpu.roll(t, HEAD_DIM // 2, axis=1) * sin

    q = jnp.dot(h, wq_ref[...], preferred_element_type=F32)
    gq = gq_ref[...]
    for hd in range(N_HEADS):
        sl = slice(hd * HEAD_DIM, (hd + 1) * HEAD_DIM)
        q_ref[:, sl] = norm_rope(q[:, sl], gq).astype(BF16)
    k = jnp.dot(h, wk_ref[...], preferred_element_type=F32)
    gk = gk_ref[...]
    for hd in range(N_KV_HEADS):
        sl = slice(hd * HEAD_DIM, (hd + 1) * HEAD_DIM)
        k_ref[:, sl] = norm_rope(k[:, sl], gk).astype(BF16)
    vt_ref[...] = jnp.dot(h, wv_ref[...], preferred_element_type=F32).T.astype(BF16)
    a = jnp.dot(h, wa_ref[...], preferred_element_type=F32)
    g = jnp.dot(h, wg_ref[...], preferred_element_type=F32)
    z_ref[...] = (a * _sigmoid(g)).astype(BF16)
    ga = jnp.dot(h, wga_ref[...], preferred_element_type=F32)
    ga_ref[...] = _sigmoid(ga + bga_ref[...]).astype(BF16)
    gc = jnp.dot(h, wgc_ref[...], preferred_element_type=F32)
    gc_ref[...] = _sigmoid(gc + bgc_ref[...]).astype(BF16)


def _in_proj(x2, lw, cos, sin, seq, tm):
    m = x2.shape[0]
    nseq = seq // tm
    row = lambda w: pl.BlockSpec((tm, w), lambda i: (i, 0))
    rope = pl.BlockSpec((tm, HEAD_DIM), lambda i: (i % nseq, 0))
    weights = [lw["gmix"], lw["wq"], lw["wk"], lw["wv"], lw["wa"], lw["wg"], lw["wga"], lw["wgc"],
               lw["gq"], lw["gk"], lw["bga"], lw["bgc"]]
    vt_spec = pl.BlockSpec((KV_W, tm), lambda i: (i // nseq, i % nseq))
    vt_shape = jax.ShapeDtypeStruct((m // seq * KV_W, seq), BF16)
    bf = lambda w: jax.ShapeDtypeStruct((m, w), BF16)
    return pl.pallas_call(
        _inproj_kernel,
        grid=(m // tm,),
        in_specs=[row(D_MODEL)] + [_resident(w.shape) for w in weights] + [rope, rope],
        out_specs=[row(ATTN_W), row(KV_W), vt_spec, row(D_CONV), row(D_MODEL), row(D_MODEL)],
        out_shape=[bf(ATTN_W), bf(KV_W), vt_shape, bf(D_CONV), bf(D_MODEL), bf(D_MODEL)],
        compiler_params=pltpu.CompilerParams(
            dimension_semantics=("parallel",), vmem_limit_bytes=VMEM_LIMIT),
        name="in_proj",
    )(x2, *weights, cos, sin)


def _attn_kernel(q_ref, k_ref, vt_ref, o_ref, st_ref, *, tq, ck):
    seq = k_ref.shape[0]
    nc = seq // ck
    nq = seq // tq

    def queries(t):
        rows = pl.ds(pl.multiple_of(t * tq, tq), tq)
        return jnp.concatenate(
            [q_ref[rows, g * HEAD_DIM:(g + 1) * HEAD_DIM] for g in range(GROUP)], axis=0)

    def scores(c, qs):
        return lax.dot_general(k_ref[c * ck:(c + 1) * ck, :], qs, (((1,), (1,)), ((), ())),
                               preferred_element_type=F32)

    st_ref[0] = scores(0, queries(0))

    def tile(t, carry):
        qs = queries(t)
        qs_next = queries(jnp.minimum(t + 1, nq - 1))
        m = l = acc = None
        for c in range(nc):
            slot = c % 2
            st_ref[1 - slot] = scores(c + 1, qs) if c + 1 < nc else scores(0, qs_next)
            st = st_ref[slot]
            mc = jnp.max(st, axis=0, keepdims=True)
            m_new = mc if m is None else jnp.maximum(m, mc)
            p = jnp.exp2(st - m_new)
            lc = jnp.sum(p, axis=0, keepdims=True)
            pv = jnp.dot(vt_ref[:, c * ck:(c + 1) * ck], p.astype(BF16),
                         preferred_element_type=F32)
            if m is None:
                l, acc = lc, pv
            else:
                alpha = jnp.exp2(m - m_new)
                l = l * alpha + lc
                acc = acc * alpha + pv
            m = m_new
        out = acc * (1.0 / l)
        rows = pl.ds(pl.multiple_of(t * tq, tq), tq)
        for g in range(GROUP):
            o_ref[rows, g * HEAD_DIM:(g + 1) * HEAD_DIM] = (
                out[:, g * tq:(g + 1) * tq].T.astype(BF16))
        return carry

    lax.fori_loop(0, nq, tile, 0)


def _attention(q, k, vt, batch, seq, tq, ck):
    m = q.shape[0]
    assert (seq // ck) % 2 == 0, "score slots alternate per key chunk and must realign per tile"
    return pl.pallas_call(
        functools.partial(_attn_kernel, tq=tq, ck=ck),
        grid=(batch, N_KV_HEADS),
        in_specs=[
            pl.BlockSpec((seq, GROUP_W), lambda b, h: (b, h)),
            pl.BlockSpec((seq, HEAD_DIM), lambda b, h: (b, h)),
            pl.BlockSpec((HEAD_DIM, seq), lambda b, h: (b * N_KV_HEADS + h, 0)),
        ],
        out_specs=pl.BlockSpec((seq, GROUP_W), lambda b, h: (b, h)),
        out_shape=jax.ShapeDtypeStruct((m, ATTN_W), BF16),
        scratch_shapes=[pltpu.VMEM((2, ck, GROUP * tq), F32)],
        compiler_params=pltpu.CompilerParams(
            dimension_semantics=("parallel", "parallel"), vmem_limit_bytes=VMEM_LIMIT),
        name="attention",
    )(q, k, vt)


def _conv_merge_kernel(zp_ref, zc_ref, zn_ref, a_ref, ga_ref, gc_ref, x_ref,
                       wdw_ref, bdw_ref, lng_ref, lnb_ref, wpw_ref, wout_ref,
                       o_ref, win_ref, conv_ref, *, ts):
    i = pl.program_id(1)
    first = i == 0
    last = i == pl.num_programs(1) - 1
    win_ref[0:HALO, :] = jnp.where(first, 0.0, zp_ref[...].astype(F32))
    win_ref[HALO:HALO + ts, :] = zc_ref[...].astype(F32)
    win_ref[HALO + ts:, :] = jnp.where(last, 0.0, zn_ref[...].astype(F32))

    off0 = HALO - CONV_PAD
    rows = min(ts, CONV_ROWS)
    for c in range(D_CONV // LANES):
        ln = slice(c * LANES, (c + 1) * LANES)
        bias = bdw_ref[:, ln]
        for r in range(ts // rows):
            t0 = r * rows
            out = None
            for j in range(SUBLANES):
                part = None
                for a in range(2 * HALO // SUBLANES):
                    kk = SUBLANES * a + j - off0
                    if 0 <= kk < CONV_K:
                        lo = t0 + SUBLANES * a
                        term = win_ref[lo:lo + rows + SUBLANES, ln] * wdw_ref[kk:kk + 1, ln]
                        part = term if part is None else part + term
                part = part[j:j + rows, :]
                out = part if out is None else out + part
            conv_ref[t0:t0 + rows, ln] = out + bias

    y = conv_ref[...]
    mu = jnp.mean(y, axis=-1, keepdims=True)
    yc = y - mu
    var = jnp.mean(yc * yc, axis=-1, keepdims=True)
    y = yc * lax.rsqrt(var + EPS) * lng_ref[...] + lnb_ref[...]
    y = (y * _sigmoid(y)).astype(BF16)
    cbr = jnp.dot(y, wpw_ref[...], preferred_element_type=F32)
    mix = ga_ref[...].astype(F32) * a_ref[...].astype(F32) + gc_ref[...].astype(F32) * cbr
    o_ref[...] = x_ref[...] + jnp.dot(mix.astype(BF16), wout_ref[...], preferred_element_type=F32)


def _conv_merge(z, a, ga, gc, x2, lw, batch, seq, ts):
    m = x2.shape[0]
    ns = seq // ts
    hb = ts // HALO
    nhalo = m // HALO
    row = lambda w: pl.BlockSpec((ts, w), lambda b, i: (b * ns + i, 0))
    prev = pl.BlockSpec((HALO, D_CONV), lambda b, i: (jnp.maximum((b * ns + i) * hb - 1, 0), 0))
    nxt = pl.BlockSpec((HALO, D_CONV),
                       lambda b, i: (jnp.minimum((b * ns + i + 1) * hb, nhalo - 1), 0))
    weights = [lw["wdw"], lw["bdw"], lw["lng"], lw["lnb"], lw["wpw"], lw["wout"]]
    return pl.pallas_call(
        functools.partial(_conv_merge_kernel, ts=ts),
        grid=(batch, ns),
        in_specs=[prev, row(D_CONV), nxt, row(ATTN_W), row(D_MODEL), row(D_MODEL), row(D_MODEL)]
                 + [_resident(w.shape) for w in weights],
        out_specs=row(D_MODEL),
        out_shape=jax.ShapeDtypeStruct((m, D_MODEL), F32),
        scratch_shapes=[pltpu.VMEM((ts + 2 * HALO, D_CONV), F32), pltpu.VMEM((ts, D_CONV), F32)],
        compiler_params=pltpu.CompilerParams(
            dimension_semantics=("parallel", "arbitrary"), vmem_limit_bytes=VMEM_LIMIT),
        name="conv_merge",
    )(z, z, z, a, ga, gc, x2, *weights)


def _ffn_kernel(x_ref, g_ref, wgate_ref, wup_ref, wdown_ref, gfin_ref, o_ref, *, final):
    x = x_ref[...]
    h = _rms(x, g_ref[...]).astype(BF16)
    gate = jnp.dot(h, wgate_ref[...], preferred_element_type=F32)
    up = jnp.dot(h, wup_ref[...], preferred_element_type=F32)
    act = (gate * _sigmoid(gate) * up).astype(BF16)
    y = x + jnp.dot(act, wdown_ref[...], preferred_element_type=F32)
    if final:
        y = _rms(y, gfin_ref[...])
    o_ref[...] = y


def _ffn(x2, lw, gfin, final, tm):
    m = x2.shape[0]
    row = pl.BlockSpec((tm, D_MODEL), lambda i: (i, 0))
    weights = [lw["gffn"], lw["wgate"], lw["wup"], lw["wdown"], gfin]
    return pl.pallas_call(
        functools.partial(_ffn_kernel, final=final),
        grid=(m // tm,),
        in_specs=[row] + [_resident(w.shape) for w in weights],
        out_specs=row,
        out_shape=jax.ShapeDtypeStruct((m, D_MODEL), F32),
        compiler_params=pltpu.CompilerParams(
            dimension_semantics=("parallel",), vmem_limit_bytes=VMEM_LIMIT),
        name="ffn",
    )(x2, *weights)


def _head_perm():
    q = HEAD_DIM // 4
    return np.concatenate([np.arange(0, q), np.arange(2 * q, 3 * q),
                           np.arange(q, 2 * q), np.arange(3 * q, 4 * q)])


def _rope_tables(seq):
    quarter = HEAD_DIM // 4
    t = jnp.arange(seq)
    row = (t // GRID_W).astype(F32)
    col = (t % GRID_W).astype(F32)
    inv = ROPE_THETA ** (-jnp.arange(quarter, dtype=F32) / quarter)
    ar = row[:, None] * inv
    ac = col[:, None] * inv
    cos = jnp.concatenate([jnp.cos(ar), jnp.cos(ac), jnp.cos(ar), jnp.cos(ac)], axis=1)
    sin = jnp.concatenate([-jnp.sin(ar), -jnp.sin(ac), jnp.sin(ar), jnp.sin(ac)], axis=1)
    return cos, sin


def _layer_weights(l, g_mix, w_in, g_q, g_k, w_dw, b_dw, ln_g, ln_b, w_pw, b_gate,
                   w_out, g_ffn, w_gu, w_down):
    perm = _head_perm()
    w = w_in[l]
    o_k, o_v, o_a = ATTN_W, ATTN_W + KV_W, ATTN_W + 2 * KV_W
    o_g, o_ga, o_gc = o_a + D_CONV, o_a + 2 * D_CONV, o_a + 2 * D_CONV + D_MODEL
    qcols = (np.arange(N_HEADS)[:, None] * HEAD_DIM + perm[None, :]).reshape(-1)
    kcols = o_k + (np.arange(N_KV_HEADS)[:, None] * HEAD_DIM + perm[None, :]).reshape(-1)
    r = lambda v: v.reshape(1, -1)
    return dict(
        gmix=r(g_mix[l]),
        wq=w[:, qcols].astype(BF16), wk=w[:, kcols].astype(BF16),
        wv=w[:, o_v:o_a].astype(BF16), wa=w[:, o_a:o_g].astype(BF16),
        wg=w[:, o_g:o_ga].astype(BF16), wga=w[:, o_ga:o_gc].astype(BF16),
        wgc=w[:, o_gc:].astype(BF16),
        gq=r(g_q[l][perm] * (LOG2E * HEAD_DIM ** -0.5)), gk=r(g_k[l][perm]),
        bga=r(b_gate[l][:D_MODEL]), bgc=r(b_gate[l][D_MODEL:]),
        wdw=w_dw[l], bdw=r(b_dw[l]), lng=r(ln_g[l]), lnb=r(ln_b[l]),
        wpw=w_pw[l].astype(BF16), wout=w_out[l].astype(BF16),
        gffn=r(g_ffn[l]), wgate=w_gu[l][:, :D_FF].astype(BF16),
        wup=w_gu[l][:, D_FF:].astype(BF16), wdown=w_down[l].astype(BF16),
    )


def _tiles(seq):
    return dict(tm=min(seq, 512), tq=min(seq, 256), ck=min(seq, 512), ts=min(seq, 256),
                tf=min(seq, 256))


def _trunk(x, layers, gfin):
    batch, seq, _ = x.shape
    t = _tiles(seq)
    cos, sin = _rope_tables(seq)
    x2 = x.reshape(batch * seq, D_MODEL)
    for l, lw in enumerate(layers):
        q, k, vt, z, ga, gc = _in_proj(x2, lw, cos, sin, seq, t["tm"])
        a = _attention(q, k, vt, batch, seq, t["tq"], t["ck"])
        x2 = _conv_merge(z, a, ga, gc, x2, lw, batch, seq, t["ts"])
        x2 = _ffn(x2, lw, gfin, l == len(layers) - 1, t["tf"])
    return x2.reshape(batch, seq, D_MODEL)


def kernel(x_prompt, x_sample, g_mix, w_in, g_q, g_k, w_dw, b_dw, ln_g, ln_b, w_pw, b_gate,
           w_out, g_ffn, w_gu, w_down, g_final):
    layers = [_layer_weights(l, g_mix, w_in, g_q, g_k, w_dw, b_dw, ln_g, ln_b, w_pw, b_gate,
                             w_out, g_ffn, w_gu, w_down) for l in range(w_in.shape[0])]
    gfin = g_final.reshape(1, -1)
    return (_trunk(x_prompt, layers, gfin), _trunk(x_sample, layers, gfin))
```

```python
import functools

import numpy as np
import jax
import jax.numpy as jnp
from jax import lax
from jax.experimental import pallas as pl
from jax.experimental.pallas import tpu as pltpu

D_MODEL = 1024
N_HEADS = 8
N_KV_HEADS = 2
HEAD_DIM = 128
GROUP = N_HEADS // N_KV_HEADS
ATTN_W = N_HEADS * HEAD_DIM
KV_W = N_KV_HEADS * HEAD_DIM
GROUP_W = GROUP * HEAD_DIM
ROPE_THETA = 10000.0
GRID_W = 64
D_CONV = 1024
CONV_K = 31
CONV_PAD = CONV_K // 2
D_FF = 2816
EPS = 1e-6
LOG2E = 1.4426950408889634

HALO = 16
LANES = 128
SUBLANES = 8
PHASES = 4
VMEM_LIMIT = 56 * 1024 * 1024

F32 = jnp.float32
BF16 = jnp.bfloat16


def _rms(x, g):
    return x * lax.rsqrt(jnp.mean(x * x, axis=-1, keepdims=True) + EPS) * g


def _sigmoid(x):
    return 1.0 / (1.0 + jnp.exp(-x))


def _resident(shape):
    nd = len(shape)
    return pl.BlockSpec(shape, lambda *_: (0,) * nd, pipeline_mode=pl.Buffered(1))


def _inproj_kernel(x_ref, gmix_ref, wq_ref, wk_ref, wv_ref, wa_ref, wg_ref, wga_ref, wgc_ref,
                   gq_ref, gk_ref, bga_ref, bgc_ref, cos_ref, sin_ref,
                   q_ref, k_ref, vt_ref, z_ref, ga_ref, gc_ref):
    h = _rms(x_ref[...], gmix_ref[...]).astype(BF16)
    cos = cos_ref[...]
    sin = sin_ref[...]

    def norm_rope(t, g):
        t = _rms(t, g)
        return t * cos + pltpu.roll(t, HEAD_DIM // 2, axis=1) * sin

    q = jnp.dot(h, wq_ref[...], preferred_element_type=F32)
    gq = gq_ref[...]
    for hd in range(N_HEADS):
        sl = slice(hd * HEAD_DIM, (hd + 1) * HEAD_DIM)
        q_ref[:, sl] = norm_rope(q[:, sl], gq).astype(BF16)
    k = jnp.dot(h, wk_ref[...], preferred_element_type=F32)
    gk = gk_ref[...]
    for hd in range(N_KV_HEADS):
        sl = slice(hd * HEAD_DIM, (hd + 1) * HEAD_DIM)
        k_ref[:, sl] = norm_rope(k[:, sl], gk).astype(BF16)
    vt_ref[...] = jnp.dot(h, wv_ref[...], preferred_element_type=F32).T.astype(BF16)
    a = jnp.dot(h, wa_ref[...], preferred_element_type=F32)
    g = jnp.dot(h, wg_ref[...], preferred_element_type=F32)
    z_ref[...] = (a * _sigmoid(g)).astype(BF16)
    ga = jnp.dot(h, wga_ref[...], preferred_element_type=F32)
    ga_ref[...] = _sigmoid(ga + bga_ref[...]).astype(BF16)
    gc = jnp.dot(h, wgc_ref[...], preferred_element_type=F32)
    gc_ref[...] = _sigmoid(gc + bgc_ref[...]).astype(BF16)


def _in_proj(x2, lw, cos, sin, seq, tm):
    m = x2.shape[0]
    nseq = seq // tm
    row = lambda w: pl.BlockSpec((tm, w), lambda i: (i, 0))
    rope = pl.BlockSpec((tm, HEAD_DIM), lambda i: (i % nseq, 0))
    weights = [lw["gmix"], lw["wq"], lw["wk"], lw["wv"], lw["wa"], lw["wg"], lw["wga"], lw["wgc"],
               lw["gq"], lw["gk"], lw["bga"], lw["bgc"]]
    vt_spec = pl.BlockSpec((KV_W, tm), lambda i: (i // nseq, i % nseq))
    vt_shape = jax.ShapeDtypeStruct((m // seq * KV_W, seq), BF16)
    bf = lambda w: jax.ShapeDtypeStruct((m, w), BF16)
    return pl.pallas_call(
        _inproj_kernel,
        grid=(m // tm,),
        in_specs=[row(D_MODEL)] + [_resident(w.shape) for w in weights] + [rope, rope],
        out_specs=[row(ATTN_W), row(KV_W), vt_spec, row(D_CONV), row(D_MODEL), row(D_MODEL)],
        out_shape=[bf(ATTN_W), bf(KV_W), vt_shape, bf(D_CONV), bf(D_MODEL), bf(D_MODEL)],
        compiler_params=pltpu.CompilerParams(
            dimension_semantics=("parallel",), vmem_limit_bytes=VMEM_LIMIT),
        name="in_proj",
    )(x2, *weights, cos, sin)


def _attn_kernel(q_ref, k_ref, vt_ref, o_ref, st_ref, *, tq, ck):
    seq = k_ref.shape[0]
    nc = seq // ck
    nq = seq // tq

    def queries(t):
        rows = pl.ds(pl.multiple_of(t * tq, tq), tq)
        return jnp.concatenate(
            [q_ref[rows, g * HEAD_DIM:(g + 1) * HEAD_DIM] for g in range(GROUP)], axis=0)

    def scores(c, qs):
        return lax.dot_general(k_ref[c * ck:(c + 1) * ck, :], qs, (((1,), (1,)), ((), ())),
                               preferred_element_type=F32)

    st_ref[0] = scores(0, queries(0))

    def tile(t, carry):
        qs = queries(t)
        qs_next = queries(jnp.minimum(t + 1, nq - 1))
        m = l = acc = None
        for c in range(nc):
            slot = c % 2
            st_ref[1 - slot] = scores(c + 1, qs) if c + 1 < nc else scores(0, qs_next)
            st = st_ref[slot]
            mc = jnp.max(st, axis=0, keepdims=True)
            m_new = mc if m is None else jnp.maximum(m, mc)
            p = jnp.exp2(st - m_new)
            lc = jnp.sum(p, axis=0, keepdims=True)
            pv = jnp.dot(vt_ref[:, c * ck:(c + 1) * ck], p.astype(BF16),
                         preferred_element_type=F32)
            if m is None:
                l, acc = lc, pv
            else:
                alpha = jnp.exp2(m - m_new)
                l = l * alpha + lc
                acc = acc * alpha + pv
            m = m_new
        out = acc * (1.0 / l)
        rows = pl.ds(pl.multiple_of(t * tq, tq), tq)
        for g in range(GROUP):
            o_ref[rows, g * HEAD_DIM:(g + 1) * HEAD_DIM] = (
                out[:, g * tq:(g + 1) * tq].T.astype(BF16))
        return carry

    lax.fori_loop(0, nq, tile, 0)


def _attention(q, k, vt, batch, seq, tq, ck):
    m = q.shape[0]
    assert (seq // ck) % 2 == 0, "score slots alternate per key chunk and must realign per tile"
    return pl.pallas_call(
        functools.partial(_attn_kernel, tq=tq, ck=ck),
        grid=(batch, N_KV_HEADS),
        in_specs=[
            pl.BlockSpec((seq, GROUP_W), lambda b, h: (b, h)),
            pl.BlockSpec((seq, HEAD_DIM), lambda b, h: (b, h)),
            pl.BlockSpec((HEAD_DIM, seq), lambda b, h: (b * N_KV_HEADS + h, 0)),
        ],
        out_specs=pl.BlockSpec((seq, GROUP_W), lambda b, h: (b, h)),
        out_shape=jax.ShapeDtypeStruct((m, ATTN_W), BF16),
        scratch_shapes=[pltpu.VMEM((2, ck, GROUP * tq), F32)],
        compiler_params=pltpu.CompilerParams(
            dimension_semantics=("parallel", "parallel"), vmem_limit_bytes=VMEM_LIMIT),
        name="attention",
    )(q, k, vt)


def _conv_merge_kernel(zp_ref, zc_ref, zn_ref, a_ref, ga_ref, gc_ref, x_ref,
                       wdw_ref, bdw_ref, lng_ref, lnb_ref, wpw_ref, wout_ref,
                       o_ref, win_ref, conv_ref, *, ts):
    i = pl.program_id(1)
    first = i == 0
    last = i == pl.num_programs(1) - 1
    nl = D_CONV // LANES
    zp = jnp.where(first, 0.0, zp_ref[...].astype(F32))
    zc = zc_ref[...].astype(F32)
    zn = jnp.where(last, 0.0, zn_ref[...].astype(F32))
    for c in range(nl):
        ln = slice(c * LANES, (c + 1) * LANES)
        win_ref[c, 0:HALO, :] = zp[:, ln]
        win_ref[c, HALO:HALO + ts, :] = zc[:, ln]
        win_ref[c, HALO + ts:, :] = zn[:, ln]

    off0 = HALO - CONV_PAD
    blk = SUBLANES * PHASES
    for c in range(nl):
        ln = slice(c * LANES, (c + 1) * LANES)
        taps = [wdw_ref[kk:kk + 1, ln] for kk in range(CONV_K)]
        bias = bdw_ref[:, ln]
        for t0 in range(0, ts, blk):
            acc = [None] * PHASES
            for sft in range(CONV_K + PHASES - 1):
                x = win_ref.at[c][pl.ds(t0 + sft + off0, SUBLANES, stride=PHASES), :]
                for r in range(PHASES):
                    kk = sft - r
                    if 0 <= kk < CONV_K:
                        term = x * taps[kk]
                        acc[r] = term if acc[r] is None else acc[r] + term
            for r in range(PHASES):
                conv_ref.at[c][pl.ds(t0 + r, SUBLANES, stride=PHASES), :] = acc[r] + bias

    y = jnp.concatenate([conv_ref[c] for c in range(nl)], axis=1)
    mu = jnp.mean(y, axis=-1, keepdims=True)
    yc = y - mu
    var = jnp.mean(yc * yc, axis=-1, keepdims=True)
    y = yc * lax.rsqrt(var + EPS) * lng_ref[...] + lnb_ref[...]
    y = (y * _sigmoid(y)).astype(BF16)
    cbr = jnp.dot(y, wpw_ref[...], preferred_element_type=F32)
    mix = ga_ref[...].astype(F32) * a_ref[...].astype(F32) + gc_ref[...].astype(F32) * cbr
    o_ref[...] = x_ref[...] + jnp.dot(mix.astype(BF16), wout_ref[...], preferred_element_type=F32)


def _conv_merge(z, a, ga, gc, x2, lw, batch, seq, ts):
    m = x2.shape[0]
    ns = seq // ts
    hb = ts // HALO
    nhalo = m // HALO
    row = lambda w: pl.BlockSpec((ts, w), lambda b, i: (b * ns + i, 0))
    prev = pl.BlockSpec((HALO, D_CONV), lambda b, i: (jnp.maximum((b * ns + i) * hb - 1, 0), 0))
    nxt = pl.BlockSpec((HALO, D_CONV),
                       lambda b, i: (jnp.minimum((b * ns + i + 1) * hb, nhalo - 1), 0))
    weights = [lw["wdw"], lw["bdw"], lw["lng"], lw["lnb"], lw["wpw"], lw["wout"]]
    return pl.pallas_call(
        functools.partial(_conv_merge_kernel, ts=ts),
        grid=(batch, ns),
        in_specs=[prev, row(D_CONV), nxt, row(ATTN_W), row(D_MODEL), row(D_MODEL), row(D_MODEL)]
                 + [_resident(w.shape) for w in weights],
        out_specs=row(D_MODEL),
        out_shape=jax.ShapeDtypeStruct((m, D_MODEL), F32),
        scratch_shapes=[pltpu.VMEM((D_CONV // LANES, ts + 2 * HALO, LANES), F32),
                        pltpu.VMEM((D_CONV // LANES, ts, LANES), F32)],
        compiler_params=pltpu.CompilerParams(
            dimension_semantics=("parallel", "arbitrary"), vmem_limit_bytes=VMEM_LIMIT),
        name="conv_merge",
    )(z, z, z, a, ga, gc, x2, *weights)


def _ffn_kernel(x_ref, g_ref, wgate_ref, wup_ref, wdown_ref, gfin_ref, o_ref, *, final):
    x = x_ref[...]
    h = _rms(x, g_ref[...]).astype(BF16)
    gate = jnp.dot(h, wgate_ref[...], preferred_element_type=F32)
    up = jnp.dot(h, wup_ref[...], preferred_element_type=F32)
    act = (gate * _sigmoid(gate) * up).astype(BF16)
    y = x + jnp.dot(act, wdown_ref[...], preferred_element_type=F32)
    if final:
        y = _rms(y, gfin_ref[...])
    o_ref[...] = y


def _ffn(x2, lw, gfin, final, tm):
    m = x2.shape[0]
    row = pl.BlockSpec((tm, D_MODEL), lambda i: (i, 0))
    weights = [lw["gffn"], lw["wgate"], lw["wup"], lw["wdown"], gfin]
    return pl.pallas_call(
        functools.partial(_ffn_kernel, final=final),
        grid=(m // tm,),
        in_specs=[row] + [_resident(w.shape) for w in weights],
        out_specs=row,
        out_shape=jax.ShapeDtypeStruct((m, D_MODEL), F32),
        compiler_params=pltpu.CompilerParams(
            dimension_semantics=("parallel",), vmem_limit_bytes=VMEM_LIMIT),
        name="ffn",
    )(x2, *weights)


def _head_perm():
    q = HEAD_DIM // 4
    return np.concatenate([np.arange(0, q), np.arange(2 * q, 3 * q),
                           np.arange(q, 2 * q), np.arange(3 * q, 4 * q)])


def _rope_tables(seq):
    quarter = HEAD_DIM // 4
    t = jnp.arange(seq)
    row = (t // GRID_W).astype(F32)
    col = (t % GRID_W).astype(F32)
    inv = ROPE_THETA ** (-jnp.arange(quarter, dtype=F32) / quarter)
    ar = row[:, None] * inv
    ac = col[:, None] * inv
    cos = jnp.concatenate([jnp.cos(ar), jnp.cos(ac), jnp.cos(ar), jnp.cos(ac)], axis=1)
    sin = jnp.concatenate([-jnp.sin(ar), -jnp.sin(ac), jnp.sin(ar), jnp.sin(ac)], axis=1)
    return cos, sin


def _layer_weights(l, g_mix, w_in, g_q, g_k, w_dw, b_dw, ln_g, ln_b, w_pw, b_gate,
                   w_out, g_ffn, w_gu, w_down):
    perm = _head_perm()
    w = w_in[l]
    o_k, o_v, o_a = ATTN_W, ATTN_W + KV_W, ATTN_W + 2 * KV_W
    o_g, o_ga, o_gc = o_a + D_CONV, o_a + 2 * D_CONV, o_a + 2 * D_CONV + D_MODEL
    qcols = (np.arange(N_HEADS)[:, None] * HEAD_DIM + perm[None, :]).reshape(-1)
    kcols = o_k + (np.arange(N_KV_HEADS)[:, None] * HEAD_DIM + perm[None, :]).reshape(-1)
    r = lambda v: v.reshape(1, -1)
    return dict(
        gmix=r(g_mix[l]),
        wq=w[:, qcols].astype(BF16), wk=w[:, kcols].astype(BF16),
        wv=w[:, o_v:o_a].astype(BF16), wa=w[:, o_a:o_g].astype(BF16),
        wg=w[:, o_g:o_ga].astype(BF16), wga=w[:, o_ga:o_gc].astype(BF16),
        wgc=w[:, o_gc:].astype(BF16),
        gq=r(g_q[l][perm] * (LOG2E * HEAD_DIM ** -0.5)), gk=r(g_k[l][perm]),
        bga=r(b_gate[l][:D_MODEL]), bgc=r(b_gate[l][D_MODEL:]),
        wdw=w_dw[l], bdw=r(b_dw[l]), lng=r(ln_g[l]), lnb=r(ln_b[l]),
        wpw=w_pw[l].astype(BF16), wout=w_out[l].astype(BF16),
        gffn=r(g_ffn[l]), wgate=w_gu[l][:, :D_FF].astype(BF16),
        wup=w_gu[l][:, D_FF:].astype(BF16), wdown=w_down[l].astype(BF16),
    )


def _tiles(seq):
    return dict(tm=min(seq, 512), tq=min(seq, 256), ck=min(seq, 512), ts=min(seq, 256),
                tf=min(seq, 512))


def _trunk(x, layers, gfin):
    batch, seq, _ = x.shape
    t = _tiles(seq)
    cos, sin = _rope_tables(seq)
    x2 = x.reshape(batch * seq, D_MODEL)
    for l, lw in enumerate(layers):
        q, k, vt, z, ga, gc = _in_proj(x2, lw, cos, sin, seq, t["tm"])
        a = _attention(q, k, vt, batch, seq, t["tq"], t["ck"])
        x2 = _conv_merge(z, a, ga, gc, x2, lw, batch, seq, t["ts"])
        x2 = _ffn(x2, lw, gfin, l == len(layers) - 1, t["tf"])
    return x2.reshape(batch, seq, D_MODEL)


def kernel(x_prompt, x_sample, g_mix, w_in, g_q, g_k, w_dw, b_dw, ln_g, ln_b, w_pw, b_gate,
           w_out, g_ffn, w_gu, w_down, g_final):
    layers = [_layer_weights(l, g_mix, w_in, g_q, g_k, w_dw, b_dw, ln_g, ln_b, w_pw, b_gate,
                             w_out, g_ffn, w_gu, w_down) for l in range(w_in.shape[0])]
    gfin = g_final.reshape(1, -1)
    return (_trunk(x_prompt, layers, gfin), _trunk(x_sample, layers, gfin))
```

```python
import functools

import jax
import jax.numpy as jnp
from jax import lax
from jax.experimental import pallas as pl
from jax.experimental.pallas import tpu as pltpu

D_MODEL = 1024
N_HEADS = 8
N_KV_HEADS = 2
HEAD_DIM = 128
GROUP = N_HEADS // N_KV_HEADS
ATTN_W = N_HEADS * HEAD_DIM
KV_W = N_KV_HEADS * HEAD_DIM
GROUP_W = GROUP * HEAD_DIM
ROPE_THETA = 10000.0
GRID_W = 64
D_CONV = 1024
CONV_K = 31
CONV_PAD = CONV_K // 2
D_FF = 2816
EPS = 1e-6
LOG2E = 1.4426950408889634

HALO = 16
LANES = 128
SUBLANES = 8
PHASES = 4
VMEM_LIMIT = 56 * 1024 * 1024

F32 = jnp.float32
BF16 = jnp.bfloat16


def _rms(x, g):
    return x * lax.rsqrt(jnp.mean(x * x, axis=-1, keepdims=True) + EPS) * g


def _sigmoid(x):
    return 1.0 / (1.0 + jnp.exp(-x))


def _resident(shape):
    nd = len(shape)
    return pl.BlockSpec(shape, lambda *_: (0,) * nd, pipeline_mode=pl.Buffered(1))


def _inproj_kernel(x_ref, gmix_ref, wq_ref, wk_ref, wv_ref, wa_ref, wg_ref, wga_ref, wgc_ref,
                   gq_ref, gk_ref, bga_ref, bgc_ref, cos_ref, sin_ref,
                   q_ref, k_ref, vt_ref, z_ref, ga_ref, gc_ref):
    h = _rms(x_ref[...], gmix_ref[...]).astype(BF16)
    cos = cos_ref[...]
    sin = sin_ref[...]

    def proj(w_ref):
        return jnp.dot(h, w_ref[...], preferred_element_type=F32)

    def norm_rope(t, g):
        t = _rms(t, g)
        return t * cos + pltpu.roll(t, HEAD_DIM // 2, axis=1) * sin

    q = proj(wq_ref)
    gq = gq_ref[...]
    for hd in range(N_HEADS):
        sl = slice(hd * HEAD_DIM, (hd + 1) * HEAD_DIM)
        q_ref[:, sl] = norm_rope(q[:, sl], gq).astype(BF16)
    k = proj(wk_ref)
    gk = gk_ref[...]
    for hd in range(N_KV_HEADS):
        sl = slice(hd * HEAD_DIM, (hd + 1) * HEAD_DIM)
        k_ref[:, sl] = norm_rope(k[:, sl], gk).astype(BF16)
    z_ref[...] = (proj(wa_ref) * _sigmoid(proj(wg_ref))).astype(BF16)
    ga_ref[...] = _sigmoid(proj(wga_ref) + bga_ref[...]).astype(BF16)
    gc_ref[...] = _sigmoid(proj(wgc_ref) + bgc_ref[...]).astype(BF16)
    vt_ref[...] = proj(wv_ref).T.astype(BF16)


def _in_proj(x2, lw, cos, sin, seq, tm):
    m = x2.shape[0]
    nseq = seq // tm
    row = lambda w: pl.BlockSpec((tm, w), lambda i: (i, 0))
    rope = pl.BlockSpec((tm, HEAD_DIM), lambda i: (i % nseq, 0))
    weights = [lw["gmix"], lw["wq"], lw["wk"], lw["wv"], lw["wa"], lw["wg"], lw["wga"], lw["wgc"],
               lw["gq"], lw["gk"], lw["bga"], lw["bgc"]]
    vt_spec = pl.BlockSpec((KV_W, tm), lambda i: (i // nseq, i % nseq))
    vt_shape = jax.ShapeDtypeStruct((m // seq * KV_W, seq), BF16)
    bf = lambda w: jax.ShapeDtypeStruct((m, w), BF16)
    return pl.pallas_call(
        _inproj_kernel,
        grid=(m // tm,),
        in_specs=[row(D_MODEL)] + [_resident(w.shape) for w in weights] + [rope, rope],
        out_specs=[row(ATTN_W), row(KV_W), vt_spec, row(D_CONV), row(D_MODEL), row(D_MODEL)],
        out_shape=[bf(ATTN_W), bf(KV_W), vt_shape, bf(D_CONV), bf(D_MODEL), bf(D_MODEL)],
        compiler_params=pltpu.CompilerParams(
            dimension_semantics=("parallel",), vmem_limit_bytes=VMEM_LIMIT),
        name="in_proj",
    )(x2, *weights, cos, sin)


def _attn_kernel(q_ref, k_ref, vt_ref, o_ref, st_ref, *, tq, ck):
    seq = k_ref.shape[0]
    nc = seq // ck
    nq = seq // tq

    def queries(t):
        rows = pl.ds(pl.multiple_of(t * tq, tq), tq)
        return jnp.concatenate(
            [q_ref[rows, g * HEAD_DIM:(g + 1) * HEAD_DIM] for g in range(GROUP)], axis=0)

    def scores(c, qs):
        return lax.dot_general(k_ref[c * ck:(c + 1) * ck, :], qs, (((1,), (1,)), ((), ())),
                               preferred_element_type=F32)

    st_ref[0] = scores(0, queries(0))

    def tile(t, carry):
        qs = queries(t)
        qs_next = queries(jnp.minimum(t + 1, nq - 1))
        m = l = acc = None
        for c in range(nc):
            slot = c % 2
            st_ref[1 - slot] = scores(c + 1, qs) if c + 1 < nc else scores(0, qs_next)
            st = st_ref[slot]
            mc = jnp.max(st, axis=0, keepdims=True)
            m_new = mc if m is None else jnp.maximum(m, mc)
            p = jnp.exp2(st - m_new)
            lc = jnp.sum(p, axis=0, keepdims=True)
            pv = jnp.dot(vt_ref[:, c * ck:(c + 1) * ck], p.astype(BF16),
                         preferred_element_type=F32)
            if m is None:
                l, acc = lc, pv
            else:
                alpha = jnp.exp2(m - m_new)
                l = l * alpha + lc
                acc = acc * alpha + pv
            m = m_new
        out = acc * (1.0 / l)
        rows = pl.ds(pl.multiple_of(t * tq, tq), tq)
        for g in range(GROUP):
            o_ref[rows, g * HEAD_DIM:(g + 1) * HEAD_DIM] = (
                out[:, g * tq:(g + 1) * tq].T.astype(BF16))
        return carry

    lax.fori_loop(0, nq, tile, 0)


def _attention(q, k, vt, batch, seq, tq, ck):
    m = q.shape[0]
    assert (seq // ck) % 2 == 0, "score slots alternate per key chunk and must realign per tile"
    return pl.pallas_call(
        functools.partial(_attn_kernel, tq=tq, ck=ck),
        grid=(batch, N_KV_HEADS),
        in_specs=[
            pl.BlockSpec((seq, GROUP_W), lambda b, h: (b, h)),
            pl.BlockSpec((seq, HEAD_DIM), lambda b, h: (b, h)),
            pl.BlockSpec((HEAD_DIM, seq), lambda b, h: (b * N_KV_HEADS + h, 0)),
        ],
        out_specs=pl.BlockSpec((seq, GROUP_W), lambda b, h: (b, h)),
        out_shape=jax.ShapeDtypeStruct((m, ATTN_W), BF16),
        scratch_shapes=[pltpu.VMEM((2, ck, GROUP * tq), F32)],
        compiler_params=pltpu.CompilerParams(
            dimension_semantics=("parallel", "parallel"), vmem_limit_bytes=VMEM_LIMIT),
        name="attention",
    )(q, k, vt)


def _conv_merge_kernel(zp_ref, zc_ref, zn_ref, a_ref, ga_ref, gc_ref, x_ref,
                       wdw_ref, bdw_ref, lng_ref, lnb_ref, wpw_ref, wout_ref,
                       o_ref, win_ref, conv_ref, *, ts):
    i = pl.program_id(1)
    first = i == 0
    last = i == pl.num_programs(1) - 1
    nl = D_CONV // LANES
    zp = jnp.where(first, 0.0, zp_ref[...].astype(F32))
    zc = zc_ref[...].astype(F32)
    zn = jnp.where(last, 0.0, zn_ref[...].astype(F32))
    for c in range(nl):
        ln = slice(c * LANES, (c + 1) * LANES)
        win_ref[c, 0:HALO, :] = zp[:, ln]
        win_ref[c, HALO:HALO + ts, :] = zc[:, ln]
        win_ref[c, HALO + ts:, :] = zn[:, ln]

    off0 = HALO - CONV_PAD
    blk = SUBLANES * PHASES
    for c in range(nl):
        ln = slice(c * LANES, (c + 1) * LANES)
        taps = [wdw_ref[kk:kk + 1, ln] for kk in range(CONV_K)]
        bias = bdw_ref[:, ln]
        for t0 in range(0, ts, blk):
            acc = [None] * PHASES
            for sft in range(CONV_K + PHASES - 1):
                x = win_ref.at[c][pl.ds(t0 + sft + off0, SUBLANES, stride=PHASES), :]
                for r in range(PHASES):
                    kk = sft - r
                    if 0 <= kk < CONV_K:
                        term = x * taps[kk]
                        acc[r] = term if acc[r] is None else acc[r] + term
            for r in range(PHASES):
                conv_ref.at[c][pl.ds(t0 + r, SUBLANES, stride=PHASES), :] = acc[r] + bias

    y = jnp.concatenate([conv_ref[c] for c in range(nl)], axis=1)
    mu = jnp.mean(y, axis=-1, keepdims=True)
    yc = y - mu
    var = jnp.mean(yc * yc, axis=-1, keepdims=True)
    y = yc * lax.rsqrt(var + EPS) * lng_ref[...] + lnb_ref[...]
    y = (y * _sigmoid(y)).astype(BF16)
    cbr = jnp.dot(y, wpw_ref[...], preferred_element_type=F32)
    mix = ga_ref[...].astype(F32) * a_ref[...].astype(F32) + gc_ref[...].astype(F32) * cbr
    o_ref[...] = x_ref[...] + jnp.dot(mix.astype(BF16), wout_ref[...], preferred_element_type=F32)


def _conv_merge(z, a, ga, gc, x2, lw, batch, seq, ts):
    m = x2.shape[0]
    ns = seq // ts
    hb = ts // HALO
    nhalo = m // HALO
    row = lambda w: pl.BlockSpec((ts, w), lambda b, i: (b * ns + i, 0))
    prev = pl.BlockSpec((HALO, D_CONV), lambda b, i: (jnp.maximum((b * ns + i) * hb - 1, 0), 0))
    nxt = pl.BlockSpec((HALO, D_CONV),
                       lambda b, i: (jnp.minimum((b * ns + i + 1) * hb, nhalo - 1), 0))
    weights = [lw["wdw"], lw["bdw"], lw["lng"], lw["lnb"], lw["wpw"], lw["wout"]]
    return pl.pallas_call(
        functools.partial(_conv_merge_kernel, ts=ts),
        grid=(batch, ns),
        in_specs=[prev, row(D_CONV), nxt, row(ATTN_W), row(D_MODEL), row(D_MODEL), row(D_MODEL)]
                 + [_resident(w.shape) for w in weights],
        out_specs=row(D_MODEL),
        out_shape=jax.ShapeDtypeStruct((m, D_MODEL), F32),
        scratch_shapes=[pltpu.VMEM((D_CONV // LANES, ts + 2 * HALO, LANES), F32),
                        pltpu.VMEM((D_CONV // LANES, ts, LANES), F32)],
        compiler_params=pltpu.CompilerParams(
            dimension_semantics=("parallel", "arbitrary"), vmem_limit_bytes=VMEM_LIMIT),
        name="conv_merge",
    )(z, z, z, a, ga, gc, x2, *weights)


def _ffn_kernel(x_ref, g_ref, wgate_ref, wup_ref, wdown_ref, gfin_ref, o_ref, *, final):
    x = x_ref[...]
    h = _rms(x, g_ref[...]).astype(BF16)
    gate = jnp.dot(h, wgate_ref[...], preferred_element_type=F32)
    up = jnp.dot(h, wup_ref[...], preferred_element_type=F32)
    act = (gate * _sigmoid(gate) * up).astype(BF16)
    y = x + jnp.dot(act, wdown_ref[...], preferred_element_type=F32)
    if final:
        y = _rms(y, gfin_ref[...])
    o_ref[...] = y


def _ffn(x2, lw, gfin, final, tm):
    m = x2.shape[0]
    row = pl.BlockSpec((tm, D_MODEL), lambda i: (i, 0))
    weights = [lw["gffn"], lw["wgate"], lw["wup"], lw["wdown"], gfin]
    return pl.pallas_call(
        functools.partial(_ffn_kernel, final=final),
        grid=(m // tm,),
        in_specs=[row] + [_resident(w.shape) for w in weights],
        out_specs=row,
        out_shape=jax.ShapeDtypeStruct((m, D_MODEL), F32),
        compiler_params=pltpu.CompilerParams(
            dimension_semantics=("parallel",), vmem_limit_bytes=VMEM_LIMIT),
        name="ffn",
    )(x2, *weights)


def _permute_heads(x):
    lead = x.shape[:-1]
    x = x.reshape(*lead, -1, 2, 2, HEAD_DIM // 4)
    return jnp.swapaxes(x, -3, -2).reshape(*lead, -1)


def _rope_tables(seq):
    quarter = HEAD_DIM // 4
    t = jnp.arange(seq)
    row = (t // GRID_W).astype(F32)
    col = (t % GRID_W).astype(F32)
    inv = ROPE_THETA ** (-jnp.arange(quarter, dtype=F32) / quarter)
    ar = row[:, None] * inv
    ac = col[:, None] * inv
    cos = jnp.concatenate([jnp.cos(ar), jnp.cos(ac), jnp.cos(ar), jnp.cos(ac)], axis=1)
    sin = jnp.concatenate([-jnp.sin(ar), -jnp.sin(ac), jnp.sin(ar), jnp.sin(ac)], axis=1)
    return cos, sin


def _layer_weights(l, g_mix, w_in, g_q, g_k, w_dw, b_dw, ln_g, ln_b, w_pw, b_gate,
                   w_out, g_ffn, w_gu, w_down):
    w = w_in[l]
    o_k, o_v, o_a = ATTN_W, ATTN_W + KV_W, ATTN_W + 2 * KV_W
    o_g, o_ga, o_gc = o_a + D_CONV, o_a + 2 * D_CONV, o_a + 2 * D_CONV + D_MODEL
    r = lambda v: v.reshape(1, -1)
    return dict(
        gmix=r(g_mix[l]),
        wq=_permute_heads(w[:, :o_k].astype(BF16)), wk=_permute_heads(w[:, o_k:o_v].astype(BF16)),
        wv=w[:, o_v:o_a].astype(BF16), wa=w[:, o_a:o_g].astype(BF16),
        wg=w[:, o_g:o_ga].astype(BF16), wga=w[:, o_ga:o_gc].astype(BF16),
        wgc=w[:, o_gc:].astype(BF16),
        gq=r(_permute_heads(g_q[l]) * (LOG2E * HEAD_DIM ** -0.5)), gk=r(_permute_heads(g_k[l])),
        bga=r(b_gate[l][:D_MODEL]), bgc=r(b_gate[l][D_MODEL:]),
        wdw=w_dw[l], bdw=r(b_dw[l]), lng=r(ln_g[l]), lnb=r(ln_b[l]),
        wpw=w_pw[l].astype(BF16), wout=w_out[l].astype(BF16),
        gffn=r(g_ffn[l]), wgate=w_gu[l][:, :D_FF].astype(BF16),
        wup=w_gu[l][:, D_FF:].astype(BF16), wdown=w_down[l].astype(BF16),
    )


def _tiles(seq):
    return dict(tm=min(seq, 512), tq=min(seq, 256), ck=min(seq, 512), ts=min(seq, 512),
                tf=min(seq, 512))


def _trunk(x, layers, gfin):
    batch, seq, _ = x.shape
    t = _tiles(seq)
    cos, sin = _rope_tables(seq)
    x2 = x.reshape(batch * seq, D_MODEL)
    for l, lw in enumerate(layers):
        q, k, vt, z, ga, gc = _in_proj(x2, lw, cos, sin, seq, t["tm"])
        a = _attention(q, k, vt, batch, seq, t["tq"], t["ck"])
        x2 = _conv_merge(z, a, ga, gc, x2, lw, batch, seq, t["ts"])
        x2 = _ffn(x2, lw, gfin, l == len(layers) - 1, t["tf"])
    return x2.reshape(batch, seq, D_MODEL)


def kernel(x_prompt, x_sample, g_mix, w_in, g_q, g_k, w_dw, b_dw, ln_g, ln_b, w_pw, b_gate,
           w_out, g_ffn, w_gu, w_down, g_final):
    layers = [_layer_weights(l, g_mix, w_in, g_q, g_k, w_dw, b_dw, ln_g, ln_b, w_pw, b_gate,
                             w_out, g_ffn, w_gu, w_down) for l in range(w_in.shape[0])]
    gfin = g_final.reshape(1, -1)
    return (_trunk(x_prompt, layers, gfin), _trunk(x_sample, layers, gfin))
```

```python
import functools

import jax
import jax.numpy as jnp
from jax import lax
from jax.experimental import pallas as pl
from jax.experimental.pallas import tpu as pltpu

D_MODEL = 1024
N_HEADS = 8
N_KV_HEADS = 2
HEAD_DIM = 128
GROUP = N_HEADS // N_KV_HEADS
ATTN_W = N_HEADS * HEAD_DIM
KV_W = N_KV_HEADS * HEAD_DIM
GROUP_W = GROUP * HEAD_DIM
ROPE_THETA = 10000.0
GRID_W = 64
D_CONV = 1024
CONV_K = 31
CONV_PAD = CONV_K // 2
D_FF = 2816
EPS = 1e-6
LOG2E = 1.4426950408889634
O_K = ATTN_W
O_V = O_K + KV_W
O_A = O_V + KV_W
O_G = O_A + D_CONV
O_GA = O_G + D_CONV
O_GC = O_GA + D_MODEL
D_IN = O_GC + D_MODEL

HALO = 16
LANES = 128
SUBLANES = 8
PHASES = 4
VMEM_LIMIT = 56 * 1024 * 1024

F32 = jnp.float32
BF16 = jnp.bfloat16


def _rms(x, g):
    return x * lax.rsqrt(jnp.mean(x * x, axis=-1, keepdims=True) + EPS) * g


def _sigmoid(x):
    return 1.0 / (1.0 + jnp.exp(-x))


def _resident(shape):
    nd = len(shape)
    return pl.BlockSpec(shape, lambda *_: (0,) * nd, pipeline_mode=pl.Buffered(1))


def _inproj_kernel(x_ref, gmix_ref, win_ref, gq_ref, gk_ref, bgate_ref, cos_ref, sin_ref,
                   q_ref, k_ref, vt_ref, z_ref, ga_ref, gc_ref):
    h = _rms(x_ref[...], gmix_ref[...]).astype(BF16)
    cos = cos_ref[...]
    sin = sin_ref[...]

    def proj(lo, hi):
        return jnp.dot(h, win_ref[:, lo:hi], preferred_element_type=F32)

    def norm_rope(t, g):
        t = _rms(t, g)
        return t * cos + pltpu.roll(t, HEAD_DIM // 2, axis=1) * sin

    q = proj(0, O_K)
    gq = gq_ref[...]
    for hd in range(N_HEADS):
        sl = slice(hd * HEAD_DIM, (hd + 1) * HEAD_DIM)
        q_ref[:, sl] = norm_rope(q[:, sl], gq).astype(BF16)
    k = proj(O_K, O_V)
    gk = gk_ref[...]
    for hd in range(N_KV_HEADS):
        sl = slice(hd * HEAD_DIM, (hd + 1) * HEAD_DIM)
        k_ref[:, sl] = norm_rope(k[:, sl], gk).astype(BF16)
    z_ref[...] = (proj(O_A, O_G) * _sigmoid(proj(O_G, O_GA))).astype(BF16)
    ga_ref[...] = _sigmoid(proj(O_GA, O_GC) + bgate_ref[:, :D_MODEL]).astype(BF16)
    gc_ref[...] = _sigmoid(proj(O_GC, D_IN) + bgate_ref[:, D_MODEL:]).astype(BF16)
    vt_ref[...] = proj(O_V, O_A).T.astype(BF16)


def _in_proj(x2, lw, cos, sin, seq, tm):
    m = x2.shape[0]
    nseq = seq // tm
    row = lambda w: pl.BlockSpec((tm, w), lambda i: (i, 0))
    rope = pl.BlockSpec((tm, HEAD_DIM), lambda i: (i % nseq, 0))
    weights = [lw["gmix"], lw["win"], lw["gq"], lw["gk"], lw["bgate"]]
    vt_spec = pl.BlockSpec((KV_W, tm), lambda i: (i // nseq, i % nseq))
    vt_shape = jax.ShapeDtypeStruct((m // seq * KV_W, seq), BF16)
    bf = lambda w: jax.ShapeDtypeStruct((m, w), BF16)
    return pl.pallas_call(
        _inproj_kernel,
        grid=(m // tm,),
        in_specs=[row(D_MODEL)] + [_resident(w.shape) for w in weights] + [rope, rope],
        out_specs=[row(ATTN_W), row(KV_W), vt_spec, row(D_CONV), row(D_MODEL), row(D_MODEL)],
        out_shape=[bf(ATTN_W), bf(KV_W), vt_shape, bf(D_CONV), bf(D_MODEL), bf(D_MODEL)],
        compiler_params=pltpu.CompilerParams(
            dimension_semantics=("parallel",), vmem_limit_bytes=VMEM_LIMIT),
        name="in_proj",
    )(x2, *weights, cos, sin)


def _attn_kernel(q_ref, k_ref, vt_ref, o_ref, st_ref, *, tq, ck):
    seq = k_ref.shape[0]
    nc = seq // ck
    nq = seq // tq

    def queries(t):
        rows = pl.ds(pl.multiple_of(t * tq, tq), tq)
        return jnp.concatenate(
            [q_ref[rows, g * HEAD_DIM:(g + 1) * HEAD_DIM] for g in range(GROUP)], axis=0)

    def scores(c, qs):
        return lax.dot_general(k_ref[c * ck:(c + 1) * ck, :], qs, (((1,), (1,)), ((), ())),
                               preferred_element_type=F32)

    st_ref[0] = scores(0, queries(0))

    def tile(t, carry):
        qs = queries(t)
        qs_next = queries(jnp.minimum(t + 1, nq - 1))
        m = l = acc = None
        for c in range(nc):
            slot = c % 2
            st_ref[1 - slot] = scores(c + 1, qs) if c + 1 < nc else scores(0, qs_next)
            st = st_ref[slot]
            mc = jnp.max(st, axis=0, keepdims=True)
            m_new = mc if m is None else jnp.maximum(m, mc)
            p = jnp.exp2(st - m_new)
            lc = jnp.sum(p, axis=0, keepdims=True)
            pv = jnp.dot(vt_ref[:, c * ck:(c + 1) * ck], p.astype(BF16),
                         preferred_element_type=F32)
            if m is None:
                l, acc = lc, pv
            else:
                alpha = jnp.exp2(m - m_new)
                l = l * alpha + lc
                acc = acc * alpha + pv
            m = m_new
        out = acc * (1.0 / l)
        rows = pl.ds(pl.multiple_of(t * tq, tq), tq)
        for g in range(GROUP):
            o_ref[rows, g * HEAD_DIM:(g + 1) * HEAD_DIM] = (
                out[:, g * tq:(g + 1) * tq].T.astype(BF16))
        return carry

    lax.fori_loop(0, nq, tile, 0)


def _attention(q, k, vt, batch, seq, tq, ck):
    m = q.shape[0]
    assert (seq // ck) % 2 == 0, "score slots alternate per key chunk and must realign per tile"
    return pl.pallas_call(
        functools.partial(_attn_kernel, tq=tq, ck=ck),
        grid=(batch, N_KV_HEADS),
        in_specs=[
            pl.BlockSpec((seq, GROUP_W), lambda b, h: (b, h)),
            pl.BlockSpec((seq, HEAD_DIM), lambda b, h: (b, h)),
            pl.BlockSpec((HEAD_DIM, seq), lambda b, h: (b * N_KV_HEADS + h, 0)),
        ],
        out_specs=pl.BlockSpec((seq, GROUP_W), lambda b, h: (b, h)),
        out_shape=jax.ShapeDtypeStruct((m, ATTN_W), BF16),
        scratch_shapes=[pltpu.VMEM((2, ck, GROUP * tq), F32)],
        compiler_params=pltpu.CompilerParams(
            dimension_semantics=("parallel", "parallel"), vmem_limit_bytes=VMEM_LIMIT),
        name="attention",
    )(q, k, vt)


def _conv_merge_kernel(zp_ref, zc_ref, zn_ref, a_ref, ga_ref, gc_ref, x_ref,
                       wdw_ref, bdw_ref, lng_ref, lnb_ref, wpw_ref, wout_ref,
                       o_ref, win_ref, conv_ref, *, ts):
    i = pl.program_id(1)
    first = i == 0
    last = i == pl.num_programs(1) - 1
    nl = D_CONV // LANES
    zp = jnp.where(first, 0.0, zp_ref[...].astype(F32))
    zc = zc_ref[...].astype(F32)
    zn = jnp.where(last, 0.0, zn_ref[...].astype(F32))
    for c in range(nl):
        ln = slice(c * LANES, (c + 1) * LANES)
        win_ref[c, 0:HALO, :] = zp[:, ln]
        win_ref[c, HALO:HALO + ts, :] = zc[:, ln]
        win_ref[c, HALO + ts:, :] = zn[:, ln]

    off0 = HALO - CONV_PAD
    blk = SUBLANES * PHASES
    for c in range(nl):
        ln = slice(c * LANES, (c + 1) * LANES)
        taps = [wdw_ref[kk:kk + 1, ln] for kk in range(CONV_K)]
        bias = bdw_ref[:, ln]
        for t0 in range(0, ts, blk):
            acc = [None] * PHASES
            for sft in range(CONV_K + PHASES - 1):
                x = win_ref.at[c][pl.ds(t0 + sft + off0, SUBLANES, stride=PHASES), :]
                for r in range(PHASES):
                    kk = sft - r
                    if 0 <= kk < CONV_K:
                        term = x * taps[kk]
                        acc[r] = term if acc[r] is None else acc[r] + term
            for r in range(PHASES):
                conv_ref.at[c][pl.ds(t0 + r, SUBLANES, stride=PHASES), :] = acc[r] + bias

    y = jnp.concatenate([conv_ref[c] for c in range(nl)], axis=1)
    mu = jnp.mean(y, axis=-1, keepdims=True)
    yc = y - mu
    var = jnp.mean(yc * yc, axis=-1, keepdims=True)
    y = yc * lax.rsqrt(var + EPS) * lng_ref[...] + lnb_ref[...]
    y = (y * _sigmoid(y)).astype(BF16)
    cbr = jnp.dot(y, wpw_ref[...], preferred_element_type=F32)
    mix = ga_ref[...].astype(F32) * a_ref[...].astype(F32) + gc_ref[...].astype(F32) * cbr
    o_ref[...] = x_ref[...] + jnp.dot(mix.astype(BF16), wout_ref[...], preferred_element_type=F32)


def _conv_merge(z, a, ga, gc, x2, lw, batch, seq, ts):
    m = x2.shape[0]
    ns = seq // ts
    hb = ts // HALO
    nhalo = m // HALO
    row = lambda w: pl.BlockSpec((ts, w), lambda b, i: (b * ns + i, 0))
    prev = pl.BlockSpec((HALO, D_CONV), lambda b, i: (jnp.maximum((b * ns + i) * hb - 1, 0), 0))
    nxt = pl.BlockSpec((HALO, D_CONV),
                       lambda b, i: (jnp.minimum((b * ns + i + 1) * hb, nhalo - 1), 0))
    weights = [lw["wdw"], lw["bdw"], lw["lng"], lw["lnb"], lw["wpw"], lw["wout"]]
    return pl.pallas_call(
        functools.partial(_conv_merge_kernel, ts=ts),
        grid=(batch, ns),
        in_specs=[prev, row(D_CONV), nxt, row(ATTN_W), row(D_MODEL), row(D_MODEL), row(D_MODEL)]
                 + [_resident(w.shape) for w in weights],
        out_specs=row(D_MODEL),
        out_shape=jax.ShapeDtypeStruct((m, D_MODEL), F32),
        scratch_shapes=[pltpu.VMEM((D_CONV // LANES, ts + 2 * HALO, LANES), F32),
                        pltpu.VMEM((D_CONV // LANES, ts, LANES), F32)],
        compiler_params=pltpu.CompilerParams(
            dimension_semantics=("parallel", "arbitrary"), vmem_limit_bytes=VMEM_LIMIT),
        name="conv_merge",
    )(z, z, z, a, ga, gc, x2, *weights)


def _ffn_kernel(x_ref, g_ref, wgu_ref, wdown_ref, gfin_ref, o_ref, *, final):
    x = x_ref[...]
    h = _rms(x, g_ref[...]).astype(BF16)
    gate = jnp.dot(h, wgu_ref[:, :D_FF], preferred_element_type=F32)
    up = jnp.dot(h, wgu_ref[:, D_FF:], preferred_element_type=F32)
    act = (gate * _sigmoid(gate) * up).astype(BF16)
    y = x + jnp.dot(act, wdown_ref[...], preferred_element_type=F32)
    if final:
        y = _rms(y, gfin_ref[...])
    o_ref[...] = y


def _ffn(x2, lw, gfin, final, tm):
    m = x2.shape[0]
    row = pl.BlockSpec((tm, D_MODEL), lambda i: (i, 0))
    weights = [lw["gffn"], lw["wgu"], lw["wdown"], gfin]
    return pl.pallas_call(
        functools.partial(_ffn_kernel, final=final),
        grid=(m // tm,),
        in_specs=[row] + [_resident(w.shape) for w in weights],
        out_specs=row,
        out_shape=jax.ShapeDtypeStruct((m, D_MODEL), F32),
        compiler_params=pltpu.CompilerParams(
            dimension_semantics=("parallel",), vmem_limit_bytes=VMEM_LIMIT),
        name="ffn",
    )(x2, *weights)


def _permute_heads(x):
    lead = x.shape[:-1]
    x = x.reshape(*lead, -1, 2, 2, HEAD_DIM // 4)
    return jnp.swapaxes(x, -3, -2).reshape(*lead, -1)


def _rope_tables(seq):
    quarter = HEAD_DIM // 4
    t = jnp.arange(seq)
    row = (t // GRID_W).astype(F32)
    col = (t % GRID_W).astype(F32)
    inv = ROPE_THETA ** (-jnp.arange(quarter, dtype=F32) / quarter)
    ar = row[:, None] * inv
    ac = col[:, None] * inv
    cos = jnp.concatenate([jnp.cos(ar), jnp.cos(ac), jnp.cos(ar), jnp.cos(ac)], axis=1)
    sin = jnp.concatenate([-jnp.sin(ar), -jnp.sin(ac), jnp.sin(ar), jnp.sin(ac)], axis=1)
    return cos, sin


def _layer_weights(l, g_mix, w_in, g_q, g_k, w_dw, b_dw, ln_g, ln_b, w_pw, b_gate,
                   w_out, g_ffn, w_gu, w_down):
    w = w_in[l].astype(BF16)
    r = lambda v: v.reshape(1, -1)
    return dict(
        gmix=r(g_mix[l]),
        win=jnp.concatenate([_permute_heads(w[:, :O_V]), w[:, O_V:]], axis=1),
        gq=r(_permute_heads(g_q[l]) * (LOG2E * HEAD_DIM ** -0.5)), gk=r(_permute_heads(g_k[l])),
        bgate=r(b_gate[l]),
        wdw=w_dw[l], bdw=r(b_dw[l]), lng=r(ln_g[l]), lnb=r(ln_b[l]),
        wpw=w_pw[l].astype(BF16), wout=w_out[l].astype(BF16),
        gffn=r(g_ffn[l]), wgu=w_gu[l].astype(BF16), wdown=w_down[l].astype(BF16),
    )


def _tiles(seq):
    return dict(tm=min(seq, 1024), tq=min(seq, 256), ck=min(seq, 512), ts=min(seq, 512),
                tf=min(seq, 512))


def _trunk(x, layers, gfin):
    batch, seq, _ = x.shape
    t = _tiles(seq)
    cos, sin = _rope_tables(seq)
    x2 = x.reshape(batch * seq, D_MODEL)
    for l, lw in enumerate(layers):
        q, k, vt, z, ga, gc = _in_proj(x2, lw, cos, sin, seq, t["tm"])
        a = _attention(q, k, vt, batch, seq, t["tq"], t["ck"])
        x2 = _conv_merge(z, a, ga, gc, x2, lw, batch, seq, t["ts"])
        x2 = _ffn(x2, lw, gfin, l == len(layers) - 1, t["tf"])
    return x2.reshape(batch, seq, D_MODEL)


def kernel(x_prompt, x_sample, g_mix, w_in, g_q, g_k, w_dw, b_dw, ln_g, ln_b, w_pw, b_gate,
           w_out, g_ffn, w_gu, w_down, g_final):
    layers = [_layer_weights(l, g_mix, w_in, g_q, g_k, w_dw, b_dw, ln_g, ln_b, w_pw, b_gate,
                             w_out, g_ffn, w_gu, w_down) for l in range(w_in.shape[0])]
    gfin = g_final.reshape(1, -1)
    return (_trunk(x_prompt, layers, gfin), _trunk(x_sample, layers, gfin))
```

```python
import functools

import jax
import jax.numpy as jnp
from jax import lax
from jax.experimental import pallas as pl
from jax.experimental.pallas import tpu as pltpu

D_MODEL = 1024
N_HEADS = 8
N_KV_HEADS = 2
HEAD_DIM = 128
GROUP = N_HEADS // N_KV_HEADS
ATTN_W = N_HEADS * HEAD_DIM
KV_W = N_KV_HEADS * HEAD_DIM
GROUP_W = GROUP * HEAD_DIM
ROPE_THETA = 10000.0
GRID_W = 64
D_CONV = 1024
CONV_K = 31
CONV_PAD = CONV_K // 2
D_FF = 2816
EPS = 1e-6
LOG2E = 1.4426950408889634

HALO = 16
LANES = 128
SUBLANES = 8
PHASES = 4
IN_ROW_GROUPS = 2
CONV_ROW_GROUPS = 2
FFN_ROW_GROUPS = 4
VMEM_LIMIT = 56 * 1024 * 1024

F32 = jnp.float32
BF16 = jnp.bfloat16


def _rms(x, g):
    return x * lax.rsqrt(jnp.mean(x * x, axis=-1, keepdims=True) + EPS) * g


def _sigmoid(x):
    return 1.0 / (1.0 + jnp.exp(-x))


def _row_groups(rows, groups):
    per = rows // groups
    return [slice(i * per, (i + 1) * per) for i in range(groups)]


def _resident(shape):
    nd = len(shape)
    return pl.BlockSpec(shape, lambda *_: (0,) * nd, pipeline_mode=pl.Buffered(1))


def _inproj_kernel(x_ref, gmix_ref, wq_ref, wk_ref, wv_ref, wa_ref, wg_ref, wga_ref, wgc_ref,
                   gq_ref, gk_ref, bga_ref, bgc_ref, cos_ref, sin_ref,
                   q_ref, k_ref, vt_ref, z_ref, ga_ref, gc_ref):
    for rs in _row_groups(x_ref.shape[0], IN_ROW_GROUPS):
        h = _rms(x_ref[rs, :], gmix_ref[...]).astype(BF16)
        cos = cos_ref[rs, :]
        sin = sin_ref[rs, :]

        def proj(w_ref, h=h):
            return jnp.dot(h, w_ref[...], preferred_element_type=F32)

        def norm_rope(t, g, cos=cos, sin=sin):
            t = _rms(t, g)
            return t * cos + pltpu.roll(t, HEAD_DIM // 2, axis=1) * sin

        q = proj(wq_ref)
        gq = gq_ref[...]
        for hd in range(N_HEADS):
            sl = slice(hd * HEAD_DIM, (hd + 1) * HEAD_DIM)
            q_ref[rs, sl] = norm_rope(q[:, sl], gq).astype(BF16)
        k = proj(wk_ref)
        gk = gk_ref[...]
        for hd in range(N_KV_HEADS):
            sl = slice(hd * HEAD_DIM, (hd + 1) * HEAD_DIM)
            k_ref[rs, sl] = norm_rope(k[:, sl], gk).astype(BF16)
        z_ref[rs, :] = (proj(wa_ref) * _sigmoid(proj(wg_ref))).astype(BF16)
        ga_ref[rs, :] = _sigmoid(proj(wga_ref) + bga_ref[...]).astype(BF16)
        gc_ref[rs, :] = _sigmoid(proj(wgc_ref) + bgc_ref[...]).astype(BF16)
        vt_ref[:, rs] = proj(wv_ref).T.astype(BF16)


def _in_proj(x2, lw, cos, sin, seq, tm):
    m = x2.shape[0]
    nseq = seq // tm
    row = lambda w: pl.BlockSpec((tm, w), lambda i: (i, 0))
    rope = pl.BlockSpec((tm, HEAD_DIM), lambda i: (i % nseq, 0))
    weights = [lw["gmix"], lw["wq"], lw["wk"], lw["wv"], lw["wa"], lw["wg"], lw["wga"], lw["wgc"],
               lw["gq"], lw["gk"], lw["bga"], lw["bgc"]]
    vt_spec = pl.BlockSpec((KV_W, tm), lambda i: (i // nseq, i % nseq))
    vt_shape = jax.ShapeDtypeStruct((m // seq * KV_W, seq), BF16)
    bf = lambda w: jax.ShapeDtypeStruct((m, w), BF16)
    return pl.pallas_call(
        _inproj_kernel,
        grid=(m // tm,),
        in_specs=[row(D_MODEL)] + [_resident(w.shape) for w in weights] + [rope, rope],
        out_specs=[row(ATTN_W), row(KV_W), vt_spec, row(D_CONV), row(D_MODEL), row(D_MODEL)],
        out_shape=[bf(ATTN_W), bf(KV_W), vt_shape, bf(D_CONV), bf(D_MODEL), bf(D_MODEL)],
        compiler_params=pltpu.CompilerParams(
            dimension_semantics=("parallel",), vmem_limit_bytes=VMEM_LIMIT),
        name="in_proj",
    )(x2, *weights, cos, sin)


def _attn_kernel(q_ref, k_ref, vt_ref, o_ref, st_ref, *, tq, ck):
    seq = k_ref.shape[0]
    nc = seq // ck
    nq = seq // tq

    def queries(t):
        rows = pl.ds(pl.multiple_of(t * tq, tq), tq)
        return jnp.concatenate(
            [q_ref[rows, g * HEAD_DIM:(g + 1) * HEAD_DIM] for g in range(GROUP)], axis=0)

    def scores(c, qs):
        return lax.dot_general(k_ref[c * ck:(c + 1) * ck, :], qs, (((1,), (1,)), ((), ())),
                               preferred_element_type=F32)

    st_ref[0] = scores(0, queries(0))

    def tile(t, carry):
        qs = queries(t)
        qs_next = queries(jnp.minimum(t + 1, nq - 1))
        m = l = acc = None
        for c in range(nc):
            slot = c % 2
            st_ref[1 - slot] = scores(c + 1, qs) if c + 1 < nc else scores(0, qs_next)
            st = st_ref[slot]
            mc = jnp.max(st, axis=0, keepdims=True)
            m_new = mc if m is None else jnp.maximum(m, mc)
            p = jnp.exp2(st - m_new)
            lc = jnp.sum(p, axis=0, keepdims=True)
            pv = jnp.dot(vt_ref[:, c * ck:(c + 1) * ck], p.astype(BF16),
                         preferred_element_type=F32)
            if m is None:
                l, acc = lc, pv
            else:
                alpha = jnp.exp2(m - m_new)
                l = l * alpha + lc
                acc = acc * alpha + pv
            m = m_new
        out = acc * (1.0 / l)
        rows = pl.ds(pl.multiple_of(t * tq, tq), tq)
        for g in range(GROUP):
            o_ref[rows, g * HEAD_DIM:(g + 1) * HEAD_DIM] = (
                out[:, g * tq:(g + 1) * tq].T.astype(BF16))
        return carry

    lax.fori_loop(0, nq, tile, 0)


def _attention(q, k, vt, batch, seq, tq, ck):
    m = q.shape[0]
    assert (seq // ck) % 2 == 0, "score slots alternate per key chunk and must realign per tile"
    return pl.pallas_call(
        functools.partial(_attn_kernel, tq=tq, ck=ck),
        grid=(batch, N_KV_HEADS),
        in_specs=[
            pl.BlockSpec((seq, GROUP_W), lambda b, h: (b, h)),
            pl.BlockSpec((seq, HEAD_DIM), lambda b, h: (b, h)),
            pl.BlockSpec((HEAD_DIM, seq), lambda b, h: (b * N_KV_HEADS + h, 0)),
        ],
        out_specs=pl.BlockSpec((seq, GROUP_W), lambda b, h: (b, h)),
        out_shape=jax.ShapeDtypeStruct((m, ATTN_W), BF16),
        scratch_shapes=[pltpu.VMEM((2, ck, GROUP * tq), F32)],
        compiler_params=pltpu.CompilerParams(
            dimension_semantics=("parallel", "parallel"), vmem_limit_bytes=VMEM_LIMIT),
        name="attention",
    )(q, k, vt)


def _conv_merge_kernel(zp_ref, zc_ref, zn_ref, a_ref, ga_ref, gc_ref, x_ref,
                       wdw_ref, bdw_ref, lng_ref, lnb_ref, wpw_ref, wout_ref,
                       o_ref, win_ref, conv_ref, *, ts):
    i = pl.program_id(1)
    first = i == 0
    last = i == pl.num_programs(1) - 1
    nl = D_CONV // LANES
    zp = jnp.where(first, 0.0, zp_ref[...].astype(F32))
    zc = zc_ref[...].astype(F32)
    zn = jnp.where(last, 0.0, zn_ref[...].astype(F32))
    for c in range(nl):
        ln = slice(c * LANES, (c + 1) * LANES)
        win_ref[c, 0:HALO, :] = zp[:, ln]
        win_ref[c, HALO:HALO + ts, :] = zc[:, ln]
        win_ref[c, HALO + ts:, :] = zn[:, ln]

    off0 = HALO - CONV_PAD
    blk = SUBLANES * PHASES
    for c in range(nl):
        ln = slice(c * LANES, (c + 1) * LANES)
        taps = [wdw_ref[kk:kk + 1, ln] for kk in range(CONV_K)]
        bias = bdw_ref[:, ln]
        for t0 in range(0, ts, blk):
            acc = [None] * PHASES
            for sft in range(CONV_K + PHASES - 1):
                x = win_ref.at[c][pl.ds(t0 + sft + off0, SUBLANES, stride=PHASES), :]
                for r in range(PHASES):
                    kk = sft - r
                    if 0 <= kk < CONV_K:
                        term = x * taps[kk]
                        acc[r] = term if acc[r] is None else acc[r] + term
            for r in range(PHASES):
                conv_ref.at[c][pl.ds(t0 + r, SUBLANES, stride=PHASES), :] = acc[r] + bias

    for rs in _row_groups(ts, CONV_ROW_GROUPS):
        y = jnp.concatenate([conv_ref[c, rs, :] for c in range(nl)], axis=1)
        mu = jnp.mean(y, axis=-1, keepdims=True)
        yc = y - mu
        var = jnp.mean(yc * yc, axis=-1, keepdims=True)
        y = yc * lax.rsqrt(var + EPS) * lng_ref[...] + lnb_ref[...]
        y = (y * _sigmoid(y)).astype(BF16)
        cbr = jnp.dot(y, wpw_ref[...], preferred_element_type=F32)
        mix = (ga_ref[rs, :].astype(F32) * a_ref[rs, :].astype(F32)
               + gc_ref[rs, :].astype(F32) * cbr)
        o_ref[rs, :] = x_ref[rs, :] + jnp.dot(mix.astype(BF16), wout_ref[...],
                                              preferred_element_type=F32)


def _conv_merge(z, a, ga, gc, x2, lw, batch, seq, ts):
    m = x2.shape[0]
    ns = seq // ts
    hb = ts // HALO
    nhalo = m // HALO
    row = lambda w: pl.BlockSpec((ts, w), lambda b, i: (b * ns + i, 0))
    prev = pl.BlockSpec((HALO, D_CONV), lambda b, i: (jnp.maximum((b * ns + i) * hb - 1, 0), 0))
    nxt = pl.BlockSpec((HALO, D_CONV),
                       lambda b, i: (jnp.minimum((b * ns + i + 1) * hb, nhalo - 1), 0))
    weights = [lw["wdw"], lw["bdw"], lw["lng"], lw["lnb"], lw["wpw"], lw["wout"]]
    return pl.pallas_call(
        functools.partial(_conv_merge_kernel, ts=ts),
        grid=(batch, ns),
        in_specs=[prev, row(D_CONV), nxt, row(ATTN_W), row(D_MODEL), row(D_MODEL), row(D_MODEL)]
                 + [_resident(w.shape) for w in weights],
        out_specs=row(D_MODEL),
        out_shape=jax.ShapeDtypeStruct((m, D_MODEL), F32),
        scratch_shapes=[pltpu.VMEM((D_CONV // LANES, ts + 2 * HALO, LANES), F32),
                        pltpu.VMEM((D_CONV // LANES, ts, LANES), F32)],
        compiler_params=pltpu.CompilerParams(
            dimension_semantics=("parallel", "arbitrary"), vmem_limit_bytes=VMEM_LIMIT),
        name="conv_merge",
    )(z, z, z, a, ga, gc, x2, *weights)


def _ffn_kernel(x_ref, g_ref, wgate_ref, wup_ref, wdown_ref, gfin_ref, o_ref, *, final):
    for rs in _row_groups(x_ref.shape[0], FFN_ROW_GROUPS):
        x = x_ref[rs, :]
        h = _rms(x, g_ref[...]).astype(BF16)
        gate = jnp.dot(h, wgate_ref[...], preferred_element_type=F32)
        up = jnp.dot(h, wup_ref[...], preferred_element_type=F32)
        act = (gate * _sigmoid(gate) * up).astype(BF16)
        y = x + jnp.dot(act, wdown_ref[...], preferred_element_type=F32)
        if final:
            y = _rms(y, gfin_ref[...])
        o_ref[rs, :] = y


def _ffn(x2, lw, gfin, final, tm):
    m = x2.shape[0]
    row = pl.BlockSpec((tm, D_MODEL), lambda i: (i, 0))
    weights = [lw["gffn"], lw["wgate"], lw["wup"], lw["wdown"], gfin]
    return pl.pallas_call(
        functools.partial(_ffn_kernel, final=final),
        grid=(m // tm,),
        in_specs=[row] + [_resident(w.shape) for w in weights],
        out_specs=row,
        out_shape=jax.ShapeDtypeStruct((m, D_MODEL), F32),
        compiler_params=pltpu.CompilerParams(
            dimension_semantics=("parallel",), vmem_limit_bytes=VMEM_LIMIT),
        name="ffn",
    )(x2, *weights)


def _permute_heads(x):
    lead = x.shape[:-1]
    x = x.reshape(*lead, -1, 2, 2, HEAD_DIM // 4)
    return jnp.swapaxes(x, -3, -2).reshape(*lead, -1)


def _rope_tables(seq):
    quarter = HEAD_DIM // 4
    t = jnp.arange(seq)
    row = (t // GRID_W).astype(F32)
    col = (t % GRID_W).astype(F32)
    inv = ROPE_THETA ** (-jnp.arange(quarter, dtype=F32) / quarter)
    ar = row[:, None] * inv
    ac = col[:, None] * inv
    cos = jnp.concatenate([jnp.cos(ar), jnp.cos(ac), jnp.cos(ar), jnp.cos(ac)], axis=1)
    sin = jnp.concatenate([-jnp.sin(ar), -jnp.sin(ac), jnp.sin(ar), jnp.sin(ac)], axis=1)
    return cos, sin


def _layer_weights(l, g_mix, w_in, g_q, g_k, w_dw, b_dw, ln_g, ln_b, w_pw, b_gate,
                   w_out, g_ffn, w_gu, w_down):
    w = w_in[l]
    o_k, o_v, o_a = ATTN_W, ATTN_W + KV_W, ATTN_W + 2 * KV_W
    o_g, o_ga, o_gc = o_a + D_CONV, o_a + 2 * D_CONV, o_a + 2 * D_CONV + D_MODEL
    r = lambda v: v.reshape(1, -1)
    return dict(
        gmix=r(g_mix[l]),
        wq=_permute_heads(w[:, :o_k].astype(BF16)), wk=_permute_heads(w[:, o_k:o_v].astype(BF16)),
        wv=w[:, o_v:o_a].astype(BF16), wa=w[:, o_a:o_g].astype(BF16),
        wg=w[:, o_g:o_ga].astype(BF16), wga=w[:, o_ga:o_gc].astype(BF16),
        wgc=w[:, o_gc:].astype(BF16),
        gq=r(_permute_heads(g_q[l]) * (LOG2E * HEAD_DIM ** -0.5)), gk=r(_permute_heads(g_k[l])),
        bga=r(b_gate[l][:D_MODEL]), bgc=r(b_gate[l][D_MODEL:]),
        wdw=w_dw[l], bdw=r(b_dw[l]), lng=r(ln_g[l]), lnb=r(ln_b[l]),
        wpw=w_pw[l].astype(BF16), wout=w_out[l].astype(BF16),
        gffn=r(g_ffn[l]), wgate=w_gu[l][:, :D_FF].astype(BF16),
        wup=w_gu[l][:, D_FF:].astype(BF16), wdown=w_down[l].astype(BF16),
    )


def _tiles(seq):
    return dict(tm=min(seq, 512), tq=min(seq, 256), ck=min(seq, 512), ts=min(seq, 512),
                tf=min(seq, 512))


def _trunk(x, layers, gfin):
    batch, seq, _ = x.shape
    t = _tiles(seq)
    cos, sin = _rope_tables(seq)
    x2 = x.reshape(batch * seq, D_MODEL)
    for l, lw in enumerate(layers):
        q, k, vt, z, ga, gc = _in_proj(x2, lw, cos, sin, seq, t["tm"])
        a = _attention(q, k, vt, batch, seq, t["tq"], t["ck"])
        x2 = _conv_merge(z, a, ga, gc, x2, lw, batch, seq, t["ts"])
        x2 = _ffn(x2, lw, gfin, l == len(layers) - 1, t["tf"])
    return x2.reshape(batch, seq, D_MODEL)


def kernel(x_prompt, x_sample, g_mix, w_in, g_q, g_k, w_dw, b_dw, ln_g, ln_b, w_pw, b_gate,
           w_out, g_ffn, w_gu, w_down, g_final):
    layers = [_layer_weights(l, g_mix, w_in, g_q, g_k, w_dw, b_dw, ln_g, ln_b, w_pw, b_gate,
                             w_out, g_ffn, w_gu, w_down) for l in range(w_in.shape[0])]
    gfin = g_final.reshape(1, -1)
    return (_trunk(x_prompt, layers, gfin), _trunk(x_sample, layers, gfin))
```
